```python
import jax, jax.numpy as jnp
from jax import lax
import numpy as np

D_MODEL = 1024
BATCH = 8
SEQ = 8192
DEPTH = 2

N_A_LAYERS = DEPTH // 2
N_B_LAYERS = DEPTH - N_A_LAYERS
CONV_WIDTH = 3
SB_HEADS = 16
SB_HEAD_DIM = D_MODEL // SB_HEADS
Q_BLOCK = 128
PEER_HEADS = 8
PEER_N_KEYS = 128
PEER_N_EXPERTS = PEER_N_KEYS * PEER_N_KEYS
PEER_KEY_DIM = 128
PEER_QUERY_DIM = 2 * PEER_KEY_DIM
PEER_TOPK = 16
TOKEN_CHUNK = 128
RMS_EPS = 1e-6

kernel_name = "yoco_shortconv_stickbreaking_peer"


def rmsnorm(x, g):
    xf = x.astype(jnp.float32)
    y = xf * lax.rsqrt(jnp.mean(xf * xf, axis=-1, keepdims=True) + RMS_EPS)
    return (y * g.astype(jnp.float32)).astype(x.dtype)


def short_conv_mixer(xn, w_in, conv_k, w_out):
    L = xn.shape[1]
    b_gate, c_gate, h = jnp.split(xn @ w_in, 3, axis=-1)
    u = c_gate * h
    up = jnp.pad(u, ((0, 0), (CONV_WIDTH - 1, 0), (0, 0)))
    conv = up[:, 0:L] * conv_k[0]
    for i in range(1, CONV_WIDTH):
        conv = conv + up[:, i:i + L] * conv_k[i]
    return (b_gate * conv) @ w_out


def stick_breaking_attention(q, k, v):
    B, S, H, Dh = q.shape
    nb = S // Q_BLOCK
    scale = Dh ** -0.5
    q_blocks = q.reshape(B, nb, Q_BLOCK, H, Dh).transpose(1, 0, 3, 2, 4)
    k_pos = jnp.arange(S)

    def one_block(args):
        q_blk, blk = args
        z = jnp.einsum('bhqd,bhkd->bhqk', q_blk, k).astype(jnp.float32) * scale
        q_pos = blk * Q_BLOCK + jnp.arange(Q_BLOCK)
        causal = k_pos[None, :] < q_pos[:, None]
        log_not = jnp.where(causal, jax.nn.log_sigmoid(-z), 0.0)
        after = lax.cumsum(log_not, axis=3, reverse=True) - log_not
        w = jnp.where(causal, jnp.exp(jax.nn.log_sigmoid(z) + after), 0.0)
        return jnp.einsum('bhqk,bhkd->bhqd', w.astype(v.dtype), v)

    out = lax.map(one_block, (q_blocks, jnp.arange(nb)))
    return out.transpose(1, 0, 3, 2, 4).reshape(B, S, H * Dh)


def peer_ffn(xn, w_query, subkeys, u_tbl, v_tbl):
    B, S, D = xn.shape
    tok = xn.reshape(-1, TOKEN_CHUNK, D)

    def one_chunk(xc):
        C = xc.shape[0]
        q = (xc @ w_query).reshape(C, PEER_HEADS, 2, PEER_KEY_DIM)
        s = jnp.einsum('chpd,hpnd->chpn', q, subkeys).astype(jnp.float32)
        s1, i1 = lax.top_k(s[:, :, 0], PEER_TOPK)
        s2, i2 = lax.top_k(s[:, :, 1], PEER_TOPK)
        cand = (s1[..., :, None] + s2[..., None, :]).reshape(C, PEER_HEADS, PEER_TOPK * PEER_TOPK)
        top_s, top_c = lax.top_k(cand, PEER_TOPK)
        idx = (jnp.take_along_axis(i1, top_c // PEER_TOPK, axis=-1) * PEER_N_KEYS
               + jnp.take_along_axis(i2, top_c % PEER_TOPK, axis=-1))
        gate = jax.nn.softmax(top_s, axis=-1)
        act = jax.nn.gelu(jnp.einsum('chkd,cd->chk', u_tbl[idx], xc).astype(jnp.float32), approximate=False)
        coef = (gate * act).astype(xc.dtype)
        return jnp.einsum('chk,chkd->cd', coef, v_tbl[idx])

    return lax.map(one_chunk, tok).reshape(B, S, D)


def setup_inputs(seed: int = 0) -> dict:
    key = jax.random.key(seed)
    ks = jax.random.split(key, 16)
    D = D_MODEL
    f32 = jnp.float32
    nrm = lambda k, shape, s: jax.random.normal(k, shape, f32) * s
    return {
        "x": nrm(ks[0], (BATCH, SEQ, D), 1.0),
        "norm_mix": 1.0 + nrm(ks[1], (DEPTH, D), 0.02),
        "norm_ffn": 1.0 + nrm(ks[2], (DEPTH, D), 0.02),
        "conv_w_in": nrm(ks[3], (N_A_LAYERS, D, 3 * D), D ** -0.5),
        "conv_kernel": nrm(ks[4], (N_A_LAYERS, CONV_WIDTH, D), CONV_WIDTH ** -0.5),
        "conv_w_out": nrm(ks[5], (N_A_LAYERS, D, D), D ** -0.5),
        "norm_kv": 1.0 + nrm(ks[6], (D,), 0.02),
        "w_kv": nrm(ks[7], (D, 2 * SB_HEADS * SB_HEAD_DIM), D ** -0.5),
        "attn_w_q": nrm(ks[8], (N_B_LAYERS, D, SB_HEADS * SB_HEAD_DIM), D ** -0.5),
        "attn_w_o": nrm(ks[9], (N_B_LAYERS, SB_HEADS * SB_HEAD_DIM, D), D ** -0.5),
        "peer_w_query": nrm(ks[10], (DEPTH, D, PEER_HEADS * PEER_QUERY_DIM), D ** -0.5),
        "peer_subkeys": nrm(ks[11], (DEPTH, PEER_HEADS, 2, PEER_N_KEYS, PEER_KEY_DIM), PEER_KEY_DIM ** -0.5),
        "peer_u": nrm(ks[12], (DEPTH, PEER_N_EXPERTS, D), D ** -0.5),
        "peer_v": nrm(ks[13], (DEPTH, PEER_N_EXPERTS, D), PEER_HEADS ** -0.5),
        "norm_final": 1.0 + nrm(ks[14], (D,), 0.02),
    }


def reference(x, norm_mix, norm_ffn, conv_w_in, conv_kernel, conv_w_out, norm_kv, w_kv,
              attn_w_q, attn_w_o, peer_w_query, peer_subkeys, peer_u, peer_v, norm_final):
    B, S, D = x.shape
    h = x
    k_shared = None
    v_shared = None
    for layer in range(DEPTH):
        xn = rmsnorm(h, norm_mix[layer])
        if layer < N_A_LAYERS:
            h = h + short_conv_mixer(xn, conv_w_in[layer], conv_kernel[layer], conv_w_out[layer])
        else:
            j = layer - N_A_LAYERS
            q = (xn @ attn_w_q[j]).reshape(B, S, SB_HEADS, SB_HEAD_DIM)
            h = h + stick_breaking_attention(q, k_shared, v_shared) @ attn_w_o[j]
        h = h + peer_ffn(rmsnorm(h, norm_ffn[layer]), peer_w_query[layer], peer_subkeys[layer],
                         peer_u[layer], peer_v[layer])
        if layer == N_A_LAYERS - 1:
            k_flat, v_flat = jnp.split(rmsnorm(h, norm_kv) @ w_kv, 2, axis=-1)
            k_shared = k_flat.reshape(B, S, SB_HEADS, SB_HEAD_DIM).transpose(0, 2, 1, 3)
            v_shared = v_flat.reshape(B, S, SB_HEADS, SB_HEAD_DIM).transpose(0, 2, 1, 3)
    return rmsnorm(h, norm_final)
```

```python
import functools
import math

import jax
import jax.numpy as jnp
from jax import lax
from jax.experimental import pallas as pl
from jax.experimental.pallas import tpu as pltpu

F32 = jnp.float32
BF16 = jnp.bfloat16

RMS_EPS = 1e-6
CONV_WIDTH = 3
HEAD_DIM = 64
PEER_HEADS = 8
PEER_N_KEYS = 128
PEER_KEY_DIM = 128
PEER_TOPK = 16

SUBLANES = 8
LANES = 128
VMEM_LIMIT_BYTES = 56 * 1024 * 1024

CONV_TOKENS = 512
ROUTE_TOKENS = 256
EXPERT_TOKENS = 1024
EXPERT_BLOCK = 512
PROJ_TOKENS = 512
ATTN_BLOCK = 256
ATTN_SEG = ATTN_BLOCK // SUBLANES

_NT = (((1,), (1,)), ((), ()))


def _params(*sem):
    return pltpu.CompilerParams(dimension_semantics=sem,
                                vmem_limit_bytes=VMEM_LIMIT_BYTES)


def _rms_scale(x):
    return x * lax.rsqrt(jnp.mean(x * x, axis=-1, keepdims=True) + RMS_EPS)


def _conv_mixer_kernel(x_ref, g_ref, win_ref, ck_ref, wout_ref, o_ref, ubuf_ref,
                       *, tiles_per_seq):
    t = x_ref.shape[0]
    d = x_ref.shape[1]
    halo = SUBLANES
    first = (pl.program_id(0) % tiles_per_seq) == 0

    @pl.when(first)
    def _():
        ubuf_ref[0:halo, :] = jnp.zeros((halo, d), F32)

    @pl.when(jnp.logical_not(first))
    def _():
        ubuf_ref[0:halo, :] = ubuf_ref[t:t + halo, :]

    x = x_ref[...]
    xb = (_rms_scale(x) * g_ref[...]).astype(BF16)
    bg = jnp.dot(xb, win_ref[:, 0:d], preferred_element_type=F32)
    cg = jnp.dot(xb, win_ref[:, d:2 * d], preferred_element_type=F32)
    hv = jnp.dot(xb, win_ref[:, 2 * d:3 * d], preferred_element_type=F32)
    u = cg * hv
    ubuf_ref[halo:halo + t, :] = u
    conv = ubuf_ref[halo - 2:halo - 2 + t, :] * ck_ref[0:1, :]
    conv = conv + ubuf_ref[halo - 1:halo - 1 + t, :] * ck_ref[1:2, :]
    conv = conv + u * ck_ref[2:3, :]
    y = (bg * conv).astype(BF16)
    o_ref[...] = x + jnp.dot(y, wout_ref[...], preferred_element_type=F32)


def _conv_mixer(h, gain, w_in, conv_k, w_out, seq_len):
    n, d = h.shape
    t = min(CONV_TOKENS, seq_len)
    assert seq_len % t == 0 and conv_k.shape[0] == CONV_WIDTH
    return pl.pallas_call(
        functools.partial(_conv_mixer_kernel, tiles_per_seq=seq_len // t),
        grid=(n // t,),
        in_specs=[
            pl.BlockSpec((t, d), lambda i: (i, 0)),
            pl.BlockSpec((1, d), lambda i: (0, 0)),
            pl.BlockSpec((d, 3 * d), lambda i: (0, 0)),
            pl.BlockSpec((CONV_WIDTH, d), lambda i: (0, 0)),
            pl.BlockSpec((d, d), lambda i: (0, 0)),
        ],
        out_specs=pl.BlockSpec((t, d), lambda i: (i, 0)),
        out_shape=jax.ShapeDtypeStruct((n, d), F32),
        scratch_shapes=[pltpu.VMEM((t + 2 * SUBLANES, d), F32)],
        compiler_params=_params("arbitrary"),
        name="conv_mixer",
    )(h, gain.reshape(1, d), w_in.astype(BF16), conv_k, w_out.astype(BF16))


def _topk_rows(s, row_id, k):
    big = float(s.shape[0])
    vals, ids = [], []
    for _ in range(k):
        m = jnp.max(s, axis=0, keepdims=True)
        ix = jnp.min(jnp.where(s == m, row_id, big), axis=0, keepdims=True)
        vals.append(m)
        ids.append(ix)
        s = jnp.where(row_id == ix, -jnp.inf, s)
    return jnp.concatenate(vals, axis=0), jnp.concatenate(ids, axis=0)


def _select_rows(table, sel):
    out = jnp.zeros(sel.shape, F32)
    for a in range(table.shape[0]):
        out = out + jnp.where(sel == float(a), table[a:a + 1, :], 0.0)
    return out


def _peer_route_kernel(h_ref, g_ref, wqt_ref, sk_ref, xn_ref, gate_ref,
                       qt_s, i1_s, i2_s, gt_s, i1t_s, i2t_s, gtt_s):
    t = h_ref.shape[0]
    k = PEER_TOPK
    xb = (_rms_scale(h_ref[...]) * g_ref[...]).astype(BF16)
    xn_ref[...] = xb
    qt_s[...] = lax.dot_general(wqt_ref[...], xb, _NT,
                                preferred_element_type=F32).astype(BF16)
    key_id = lax.broadcasted_iota(jnp.int32, (PEER_N_KEYS, t), 0).astype(F32)
    cand_id = lax.broadcasted_iota(jnp.int32, (k * k, t), 0).astype(F32)

    def head_body(hd, carry):
        tops = []
        for p in range(2):
            row0 = pl.multiple_of((hd * 2 + p) * PEER_KEY_DIM, PEER_KEY_DIM)
            s = jnp.dot(sk_ref[hd, p], qt_s[pl.ds(row0, PEER_KEY_DIM), :],
                        preferred_element_type=F32)
            tops.append(_topk_rows(s, key_id, k))
        (v1, i1), (v2, i2) = tops
        cand = jnp.concatenate([v1[a:a + 1, :] + v2 for a in range(k)], axis=0)
        top_s, top_c = _topk_rows(cand, cand_id, k)
        a_sel = jnp.floor(top_c * (1.0 / k))
        b_sel = top_c - a_sel * k
        e = jnp.exp(top_s - top_s[0:1, :])
        gate = e / jnp.sum(e, axis=0, keepdims=True)
        r0 = pl.multiple_of(hd * k, k)
        i1_s[pl.ds(r0, k), :] = _select_rows(i1, a_sel)
        i2_s[pl.ds(r0, k), :] = _select_rows(i2, b_sel)
        gt_s[pl.ds(r0, k), :] = gate
        return carry

    lax.fori_loop(0, PEER_HEADS, head_body, 0)

    i1t_s[...] = i1_s[...].T
    i2t_s[...] = i2_s[...].T
    gtt_s[...] = gt_s[...].T

    sub_id = lax.broadcasted_iota(jnp.int32, (PEER_N_KEYS, PEER_HEADS * k), 0).astype(F32)

    def token_body(c, carry):
        i1r = i1t_s[pl.ds(c, 1), :]
        i2r = i2t_s[pl.ds(c, 1), :]
        gr = gtt_s[pl.ds(c, 1), :]
        a = jnp.where(sub_id == i1r, gr, 0.0).astype(BF16)
        b = jnp.where(sub_id == i2r, 1.0, 0.0).astype(BF16)
        gate_ref[c] = lax.dot_general(a, b, _NT,
                                      preferred_element_type=F32).astype(gate_ref.dtype)
        return carry

    lax.fori_loop(0, t, token_body, 0, unroll=8)


def _peer_route(h, gain, w_query, subkeys):
    n, d = h.shape
    t = min(ROUTE_TOKENS, n)
    picks = PEER_HEADS * PEER_TOPK
    qdim = PEER_HEADS * 2 * PEER_KEY_DIM
    assert subkeys.shape == (PEER_HEADS, 2, PEER_N_KEYS, PEER_KEY_DIM)
    assert w_query.shape == (d, qdim) and n % t == 0
    return pl.pallas_call(
        _peer_route_kernel,
        grid=(n // t,),
        in_specs=[
            pl.BlockSpec((t, d), lambda i: (i, 0)),
            pl.BlockSpec((1, d), lambda i: (0, 0)),
            pl.BlockSpec((qdim, d), lambda i: (0, 0)),
            pl.BlockSpec((PEER_HEADS, 2, PEER_N_KEYS, PEER_KEY_DIM), lambda i: (0, 0, 0, 0)),
        ],
        out_specs=[
            pl.BlockSpec((t, d), lambda i: (i, 0)),
            pl.BlockSpec((t, PEER_N_KEYS, PEER_N_KEYS), lambda i: (i, 0, 0)),
        ],
        out_shape=[
            jax.ShapeDtypeStruct((n, d), BF16),
            jax.ShapeDtypeStruct((n, PEER_N_KEYS, PEER_N_KEYS), BF16),
        ],
        scratch_shapes=[
            pltpu.VMEM((qdim, t), BF16),
            pltpu.VMEM((picks, t), F32),
            pltpu.VMEM((picks, t), F32),
            pltpu.VMEM((picks, t), F32),
            pltpu.VMEM((t, picks), F32),
            pltpu.VMEM((t, picks), F32),
            pltpu.VMEM((t, picks), F32),
        ],
        compiler_params=_params("arbitrary"),
        name="peer_route",
    )(h, gain.reshape(1, d), w_query.T.astype(BF16), subkeys.astype(BF16))


def _peer_expert_kernel(xn_ref, gate_ref, u_ref, v_ref, h_ref, fg_ref, o_ref, acc_ref,
                        *, final_norm):
    e = pl.program_id(1)

    @pl.when(e == 0)
    def _():
        acc_ref[...] = jnp.zeros(acc_ref.shape, F32)

    act = lax.dot_general(xn_ref[...], u_ref[...], _NT, preferred_element_type=F32)
    gelu = 0.5 * act * (1.0 + lax.erf(act * (1.0 / math.sqrt(2.0))))
    coef = (gate_ref[...].astype(F32) * gelu).astype(BF16)
    acc_ref[...] += jnp.dot(coef, v_ref[...], preferred_element_type=F32)

    @pl.when(e == pl.num_programs(1) - 1)
    def _():
        y = h_ref[...] + acc_ref[...]
        if final_norm:
            y = _rms_scale(y) * fg_ref[...]
        o_ref[...] = y


def _peer_experts(xn, gate_flat, u_tbl, v_tbl, h, final_gain, final_norm):
    n, d = h.shape
    n_exp = u_tbl.shape[0]
    t = min(EXPERT_TOKENS, n)
    eb = min(EXPERT_BLOCK, n_exp)
    assert n % t == 0 and n_exp % eb == 0
    return pl.pallas_call(
        functools.partial(_peer_expert_kernel, final_norm=final_norm),
        grid=(n // t, n_exp // eb),
        in_specs=[
            pl.BlockSpec((t, d), lambda i, e: (i, 0)),
            pl.BlockSpec((t, eb), lambda i, e: (i, e)),
            pl.BlockSpec((eb, d), lambda i, e: (e, 0)),
            pl.BlockSpec((eb, d), lambda i, e: (e, 0)),
            pl.BlockSpec((t, d), lambda i, e: (i, 0)),
            pl.BlockSpec((1, d), lambda i, e: (0, 0)),
        ],
        out_specs=pl.BlockSpec((t, d), lambda i, e: (i, 0)),
        out_shape=jax.ShapeDtypeStruct((n, d), F32),
        scratch_shapes=[pltpu.VMEM((t, d), F32)],
        compiler_params=_params("arbitrary", "arbitrary"),
        name="peer_experts",
    )(xn, gate_flat, u_tbl.astype(BF16), v_tbl.astype(BF16), h, final_gain.reshape(1, d))


def _peer_ffn(h, gain, w_query, subkeys, u_tbl, v_tbl, final_gain, final_norm):
    n = h.shape[0]
    xn, gate = _peer_route(h, gain, w_query, subkeys)
    gate_flat = gate.reshape(n, PEER_N_KEYS * PEER_N_KEYS)
    return _peer_experts(xn, gate_flat, u_tbl, v_tbl, h, final_gain, final_norm)


def _kvq_kernel(h_ref, hp_ref, gkv_ref, gq_ref, wk_ref, wvt_ref, wqt_ref,
                k_ref, vt_ref, qt_ref):
    n_heads = qt_ref.shape[1]
    t = h_ref.shape[0]
    xkv = (_rms_scale(hp_ref[...]) * gkv_ref[...]).astype(BF16)
    xq = (_rms_scale(h_ref[...]) * gq_ref[...]).astype(BF16)
    k_ref[...] = jnp.dot(xkv, wk_ref[...], preferred_element_type=F32).astype(BF16)
    vt_ref[0] = lax.dot_general(wvt_ref[...], xkv, _NT,
                                preferred_element_type=F32).astype(BF16)
    qt = (lax.dot_general(wqt_ref[...], xq, _NT, preferred_element_type=F32)
          * (HEAD_DIM ** -0.5)).astype(BF16)
    zeros = jnp.zeros((HEAD_DIM, t), BF16)
    for hd in range(n_heads):
        q_h = qt[hd * HEAD_DIM:(hd + 1) * HEAD_DIM, :]
        if hd % 2 == 0:
            qt_ref[0, hd, 0:HEAD_DIM, :] = q_h
            qt_ref[0, hd, HEAD_DIM:2 * HEAD_DIM, :] = zeros
        else:
            qt_ref[0, hd, 0:HEAD_DIM, :] = zeros
            qt_ref[0, hd, HEAD_DIM:2 * HEAD_DIM, :] = q_h


def _key_permute(x, seq_len):
    n, d = x.shape
    kb = min(ATTN_BLOCK, seq_len)
    seg = kb // SUBLANES
    return x.reshape(n // kb, SUBLANES, seg, d).swapaxes(1, 2).reshape(n, d)


def _kvq_proj(h, g_kv, g_q, w_kv, w_q, batch, seq_len):
    n, d = h.shape
    n_heads = d // HEAD_DIM
    t = min(PROJ_TOKENS, seq_len)
    tiles = seq_len // t
    assert seq_len % t == 0 and t % min(ATTN_BLOCK, seq_len) == 0 and n_heads % 2 == 0
    w_k, w_v = w_kv[:, :d], w_kv[:, d:]
    return pl.pallas_call(
        _kvq_kernel,
        grid=(n // t,),
        in_specs=[
            pl.BlockSpec((t, d), lambda i: (i, 0)),
            pl.BlockSpec((t, d), lambda i: (i, 0)),
            pl.BlockSpec((1, d), lambda i: (0, 0)),
            pl.BlockSpec((1, d), lambda i: (0, 0)),
            pl.BlockSpec((d, d), lambda i: (0, 0)),
            pl.BlockSpec((d, d), lambda i: (0, 0)),
            pl.BlockSpec((d, d), lambda i: (0, 0)),
        ],
        out_specs=[
            pl.BlockSpec((t, d), lambda i: (i, 0)),
            pl.BlockSpec((1, d, t), lambda i: (i // tiles, 0, i % tiles)),
            pl.BlockSpec((1, n_heads, 2 * HEAD_DIM, t), lambda i: (i // tiles, 0, 0, i % tiles)),
        ],
        out_shape=[
            jax.ShapeDtypeStruct((n, d), BF16),
            jax.ShapeDtypeStruct((batch, d, seq_len), BF16),
            jax.ShapeDtypeStruct((batch, n_heads, 2 * HEAD_DIM, seq_len), BF16),
        ],
        compiler_params=_params("arbitrary"),
        name="kvq_proj",
    )(h, _key_permute(h, seq_len), g_kv.reshape(1, d), g_q.reshape(1, d),
      w_k.astype(BF16), w_v.T.astype(BF16), w_q.T.astype(BF16))


def _attn_kernel(qt_ref, k_ref, vt_ref, o_ref, *, blk):
    seg = blk // SUBLANES
    qi = pl.program_id(2)
    qt = qt_ref[0, 0]
    row = lax.broadcasted_iota(jnp.int32, (blk, blk), 0)
    lane = lax.broadcasted_iota(jnp.int32, (blk, blk), 1)
    key_local = (row % SUBLANES) * seg + row // SUBLANES
    causal = key_local < lane

    def block(kb, carry, acc, masked):
        k0 = pl.multiple_of(kb * blk, blk)
        z = jnp.dot(k_ref[0, pl.ds(k0, blk), :], qt, preferred_element_type=F32)
        if masked:
            z = jnp.where(causal, z, -1e30)
        e = jnp.exp(-jnp.abs(z))
        r = 1.0 / (1.0 + e)
        er = e * r
        pos = z >= 0.0
        beta = jnp.where(pos, r, er)
        keep = jnp.where(pos, er, r)
        run = jnp.ones((SUBLANES, blk), F32)
        parts = [None] * seg
        for rr in reversed(range(seg)):
            parts[rr] = run
            run = run * keep[rr * SUBLANES:(rr + 1) * SUBLANES, :]
        offs = [None] * SUBLANES
        c = carry
        for s in reversed(range(SUBLANES)):
            offs[s] = c
            c = c * run[s:s + 1, :]
        off = jnp.concatenate(offs, axis=0)
        w = jnp.concatenate(
            [beta[rr * SUBLANES:(rr + 1) * SUBLANES, :] * (parts[rr] * off)
             for rr in range(seg)], axis=0).astype(BF16)
        acc = acc + jnp.dot(vt_ref[0, :, pl.ds(k0, blk)], w, preferred_element_type=F32)
        return c, acc

    carry0 = jnp.ones((1, blk), F32)
    acc0 = jnp.zeros((HEAD_DIM, blk), F32)
    carry, acc = block(qi, carry0, acc0, True)

    def body(j, state):
        return block(qi - 1 - j, state[0], state[1], False)

    carry, acc = lax.fori_loop(0, qi, body, (carry, acc))
    o_ref[0] = acc


def _stick_breaking_attention(qt_pad, k_perm, vt_perm, seq_len):
    batch, n_heads = qt_pad.shape[0], qt_pad.shape[1]
    d = n_heads * HEAD_DIM
    blk = min(ATTN_BLOCK, seq_len)
    k3 = k_perm.reshape(batch, seq_len, d)
    return pl.pallas_call(
        functools.partial(_attn_kernel, blk=blk),
        grid=(batch, n_heads, seq_len // blk),
        in_specs=[
            pl.BlockSpec((1, 1, 2 * HEAD_DIM, blk), lambda b, h, q: (b, h, 0, q)),
            pl.BlockSpec((1, seq_len, 2 * HEAD_DIM), lambda b, h, q: (b, 0, h // 2)),
            pl.BlockSpec((1, HEAD_DIM, seq_len), lambda b, h, q: (b, h, 0)),
        ],
        out_specs=pl.BlockSpec((1, HEAD_DIM, blk), lambda b, h, q: (b, h, q)),
        out_shape=jax.ShapeDtypeStruct((batch, d, seq_len), F32),
        compiler_params=_params("arbitrary", "arbitrary", "arbitrary"),
        name="stick_breaking_attention",
    )(qt_pad, k3, vt_perm)


def _out_proj_kernel(ot_ref, w_ref, h_ref, o_ref):
    a = ot_ref[0].T.astype(BF16)
    o_ref[...] = h_ref[...] + jnp.dot(a, w_ref[...], preferred_element_type=F32)


def _out_proj(ot, w_o, h, seq_len):
    n, d = h.shape
    t = min(PROJ_TOKENS, seq_len)
    tiles = seq_len // t
    return pl.pallas_call(
        _out_proj_kernel,
        grid=(n // t,),
        in_specs=[
            pl.BlockSpec((1, d, t), lambda i: (i // tiles, 0, i % tiles)),
            pl.BlockSpec((d, d), lambda i: (0, 0)),
            pl.BlockSpec((t, d), lambda i: (i, 0)),
        ],
        out_specs=pl.BlockSpec((t, d), lambda i: (i, 0)),
        out_shape=jax.ShapeDtypeStruct((n, d), F32),
        compiler_params=_params("arbitrary"),
        name="attn_out_proj",
    )(ot, w_o.astype(BF16), h)


def kernel(x, norm_mix, norm_ffn, conv_w_in, conv_kernel, conv_w_out, norm_kv, w_kv,
           attn_w_q, attn_w_o, peer_w_query, peer_subkeys, peer_u, peer_v, norm_final):
    batch, seq_len, d = x.shape
    assert norm_mix.shape[0] == 2 and conv_w_in.shape[0] == 1 and attn_w_q.shape[0] == 1
    h = x.reshape(batch * seq_len, d)
    h = _conv_mixer(h, norm_mix[0], conv_w_in[0], conv_kernel[0], conv_w_out[0], seq_len)
    h = _peer_ffn(h, norm_ffn[0], peer_w_query[0], peer_subkeys[0], peer_u[0], peer_v[0],
                  norm_final, final_norm=False)
    k_perm, vt_perm, qt_pad = _kvq_proj(h, norm_kv, norm_mix[1], w_kv, attn_w_q[0],
                                        batch, seq_len)
    ot = _stick_breaking_attention(qt_pad, k_perm, vt_perm, seq_len)
    h = _out_proj(ot, attn_w_o[0], h, seq_len)
    h = _peer_ffn(h, norm_ffn[1], peer_w_query[1], peer_subkeys[1], peer_u[1], peer_v[1],
                  norm_final, final_norm=True)
    return h.reshape(batch, seq_len, d)
```

```python
import functools
import math

import jax
import jax.numpy as jnp
from jax import lax
from jax.experimental import pallas as pl
from jax.experimental.pallas import tpu as pltpu

F32 = jnp.float32
BF16 = jnp.bfloat16

RMS_EPS = 1e-6
CONV_WIDTH = 3
HEAD_DIM = 64
PEER_HEADS = 8
PEER_N_KEYS = 128
PEER_KEY_DIM = 128
PEER_TOPK = 16

SUBLANES = 8
LANES = 128
VMEM_LIMIT_BYTES = 56 * 1024 * 1024

CONV_TOKENS = 512
ROUTE_TOKENS = 256
ROUTE_UNROLL = 32
EXPERT_TOKENS = 1024
EXPERT_ROWS = SUBLANES
EXPERT_BLOCK = EXPERT_ROWS * PEER_N_KEYS
EXPERT_CHUNK = 2 * PEER_N_KEYS
PROJ_TOKENS = 512
ATTN_BLOCK = 256
ATTN_SEG = ATTN_BLOCK // SUBLANES

_NT = (((1,), (1,)), ((), ()))


def _params(*sem):
    return pltpu.CompilerParams(dimension_semantics=sem,
                                vmem_limit_bytes=VMEM_LIMIT_BYTES)


def _rms_scale(x):
    return x * lax.rsqrt(jnp.mean(x * x, axis=-1, keepdims=True) + RMS_EPS)


def _conv_mixer_kernel(x_ref, g_ref, win_ref, ck_ref, wout_ref, o_ref, ubuf_ref,
                       *, tiles_per_seq):
    t = x_ref.shape[0]
    d = x_ref.shape[1]
    halo = SUBLANES
    first = (pl.program_id(0) % tiles_per_seq) == 0

    @pl.when(first)
    def _():
        ubuf_ref[0:halo, :] = jnp.zeros((halo, d), F32)

    @pl.when(jnp.logical_not(first))
    def _():
        ubuf_ref[0:halo, :] = ubuf_ref[t:t + halo, :]

    x = x_ref[...]
    xb = (_rms_scale(x) * g_ref[...]).astype(BF16)
    bg = jnp.dot(xb, win_ref[:, 0:d], preferred_element_type=F32)
    cg = jnp.dot(xb, win_ref[:, d:2 * d], preferred_element_type=F32)
    hv = jnp.dot(xb, win_ref[:, 2 * d:3 * d], preferred_element_type=F32)
    u = cg * hv
    ubuf_ref[halo:halo + t, :] = u
    conv = ubuf_ref[halo - 2:halo - 2 + t, :] * ck_ref[0:1, :]
    conv = conv + ubuf_ref[halo - 1:halo - 1 + t, :] * ck_ref[1:2, :]
    conv = conv + u * ck_ref[2:3, :]
    y = (bg * conv).astype(BF16)
    o_ref[...] = x + jnp.dot(y, wout_ref[...], preferred_element_type=F32)


def _conv_mixer(h, gain, w_in, conv_k, w_out, seq_len):
    n, d = h.shape
    t = min(CONV_TOKENS, seq_len)
    assert seq_len % t == 0 and conv_k.shape[0] == CONV_WIDTH
    return pl.pallas_call(
        functools.partial(_conv_mixer_kernel, tiles_per_seq=seq_len // t),
        grid=(n // t,),
        in_specs=[
            pl.BlockSpec((t, d), lambda i: (i, 0)),
            pl.BlockSpec((1, d), lambda i: (0, 0)),
            pl.BlockSpec((d, 3 * d), lambda i: (0, 0)),
            pl.BlockSpec((CONV_WIDTH, d), lambda i: (0, 0)),
            pl.BlockSpec((d, d), lambda i: (0, 0)),
        ],
        out_specs=pl.BlockSpec((t, d), lambda i: (i, 0)),
        out_shape=jax.ShapeDtypeStruct((n, d), F32),
        scratch_shapes=[pltpu.VMEM((t + 2 * SUBLANES, d), F32)],
        compiler_params=_params("arbitrary"),
        name="conv_mixer",
    )(h, gain.reshape(1, d), w_in.astype(BF16), conv_k, w_out.astype(BF16))


def _topk_rows(s, row_id, k, big=None):
    if big is None:
        big = float(s.shape[0])
    vals, ids = [], []
    for _ in range(k):
        m = jnp.max(s, axis=0, keepdims=True)
        ix = jnp.min(jnp.where(s == m, row_id, big), axis=0, keepdims=True)
        vals.append(m)
        ids.append(ix)
        s = jnp.where(row_id == ix, -jnp.inf, s)
    return jnp.concatenate(vals, axis=0), jnp.concatenate(ids, axis=0)


def _select_rows(table, sel):
    out = jnp.zeros(sel.shape, F32)
    for a in range(table.shape[0]):
        out = out + jnp.where(sel == float(a), table[a:a + 1, :], 0.0)
    return out


def _peer_route_kernel(h_ref, g_ref, wqt_ref, sk_ref, xn_ref, gate_ref,
                       qt_s, i1_s, i2_s, gt_s, i1t_s, i2t_s, gtt_s):
    t = h_ref.shape[0]
    k = PEER_TOPK
    xb = (_rms_scale(h_ref[...]) * g_ref[...]).astype(BF16)
    xn_ref[...] = xb
    qt_s[...] = lax.dot_general(wqt_ref[...], xb, _NT,
                                preferred_element_type=F32).astype(BF16)
    key_id = lax.broadcasted_iota(jnp.int32, (PEER_N_KEYS, t), 0).astype(F32)
    half = k // 2
    rho = lax.broadcasted_iota(jnp.int32, (k + (half - 1) * half + (k - half), t), 0)
    mid = rho - k
    cand_id = jnp.where(rho < k, rho,
                        jnp.where(mid < (half - 1) * half,
                                  (1 + mid // half) * k + mid % half,
                                  (half + mid - (half - 1) * half) * k)).astype(F32)

    def head_body(hd, carry):
        tops = []
        for p in range(2):
            row0 = pl.multiple_of((hd * 2 + p) * PEER_KEY_DIM, PEER_KEY_DIM)
            s = jnp.dot(sk_ref[hd, p], qt_s[pl.ds(row0, PEER_KEY_DIM), :],
                        preferred_element_type=F32)
            tops.append(_topk_rows(s, key_id, k))
        (v1, i1), (v2, i2) = tops
        cand = jnp.concatenate(
            [v1[0:1, :] + v2]
            + [v1[a:a + 1, :] + v2[0:half, :] for a in range(1, half)]
            + [v1[half:k, :] + v2[0:1, :]], axis=0)
        top_s, top_c = _topk_rows(cand, cand_id, k, big=float(k * k))
        a_sel = jnp.floor(top_c * (1.0 / k))
        b_sel = top_c - a_sel * k
        e = jnp.exp(top_s - top_s[0:1, :])
        gate = e / jnp.sum(e, axis=0, keepdims=True)
        r0 = pl.multiple_of(hd * k, k)
        i1_s[pl.ds(r0, k), :] = _select_rows(i1, a_sel)
        i2_s[pl.ds(r0, k), :] = _select_rows(i2, b_sel)
        gt_s[pl.ds(r0, k), :] = gate
        return carry

    lax.fori_loop(0, PEER_HEADS, head_body, 0)

    i1t_s[...] = i1_s[...].T
    i2t_s[...] = i2_s[...].T
    gtt_s[...] = gt_s[...].T

    sub_id = lax.broadcasted_iota(jnp.int32, (PEER_N_KEYS, PEER_HEADS * k), 0).astype(F32)

    def token_body(c, carry):
        i1r = i1t_s[pl.ds(c, 1), :]
        i2r = i2t_s[pl.ds(c, 1), :]
        gr = gtt_s[pl.ds(c, 1), :]
        a = jnp.where(sub_id == i1r, gr, 0.0).astype(BF16)
        b = jnp.where(sub_id == i2r, 1.0, 0.0).astype(BF16)
        g = lax.dot_general(a, b, _NT, preferred_element_type=F32)
        gate_ref[:, c, :, :] = g.reshape(PEER_N_KEYS // EXPERT_ROWS, EXPERT_ROWS, PEER_N_KEYS)
        return carry

    lax.fori_loop(0, t, token_body, 0, unroll=ROUTE_UNROLL)


def _peer_route(h, gain, w_query, subkeys):
    n, d = h.shape
    t = min(ROUTE_TOKENS, n)
    picks = PEER_HEADS * PEER_TOPK
    qdim = PEER_HEADS * 2 * PEER_KEY_DIM
    assert subkeys.shape == (PEER_HEADS, 2, PEER_N_KEYS, PEER_KEY_DIM)
    assert w_query.shape == (d, qdim) and n % t == 0
    return pl.pallas_call(
        _peer_route_kernel,
        grid=(n // t,),
        in_specs=[
            pl.BlockSpec((t, d), lambda i: (i, 0)),
            pl.BlockSpec((1, d), lambda i: (0, 0)),
            pl.BlockSpec((qdim, d), lambda i: (0, 0)),
            pl.BlockSpec((PEER_HEADS, 2, PEER_N_KEYS, PEER_KEY_DIM), lambda i: (0, 0, 0, 0)),
        ],
        out_specs=[
            pl.BlockSpec((t, d), lambda i: (i, 0)),
            pl.BlockSpec((PEER_N_KEYS // EXPERT_ROWS, t, EXPERT_ROWS, PEER_N_KEYS),
                         lambda i: (0, i, 0, 0)),
        ],
        out_shape=[
            jax.ShapeDtypeStruct((n, d), BF16),
            jax.ShapeDtypeStruct((PEER_N_KEYS // EXPERT_ROWS, n, EXPERT_ROWS, PEER_N_KEYS), F32),
        ],
        scratch_shapes=[
            pltpu.VMEM((qdim, t), BF16),
            pltpu.VMEM((picks, t), F32),
            pltpu.VMEM((picks, t), F32),
            pltpu.VMEM((picks, t), F32),
            pltpu.VMEM((t, picks), F32),
            pltpu.VMEM((t, picks), F32),
            pltpu.VMEM((t, picks), F32),
        ],
        compiler_params=_params("arbitrary"),
        name="peer_route",
    )(h, gain.reshape(1, d), w_query.T.astype(BF16), subkeys.astype(BF16))


def _peer_expert_kernel(xn_ref, gate_ref, u_ref, v_ref, h_ref, fg_ref, o_ref, coef_ref,
                        *, final_norm):
    e = pl.program_id(1)
    t = xn_ref.shape[0]

    @pl.when(e == 0)
    def _():
        o_ref[...] = h_ref[...]

    xn = xn_ref[...]
    per_chunk = EXPERT_CHUNK // PEER_N_KEYS
    for q in range(EXPERT_BLOCK // EXPERT_CHUNK):
        cols = slice(q * EXPERT_CHUNK, (q + 1) * EXPERT_CHUNK)
        act = lax.dot_general(xn, u_ref[cols, :], _NT, preferred_element_type=F32)
        gelu = 0.5 * act * (1.0 + lax.erf(act * (1.0 / math.sqrt(2.0))))
        gate = jnp.concatenate(
            [gate_ref[0, pl.ds(q * per_chunk + j, t, stride=EXPERT_ROWS), :]
             for j in range(per_chunk)], axis=1)
        coef_ref[:, cols] = (gate * gelu).astype(BF16)
    o_ref[...] += jnp.dot(coef_ref[...], v_ref[...], preferred_element_type=F32)

    if final_norm:
        @pl.when(e == pl.num_programs(1) - 1)
        def _():
            o_ref[...] = _rms_scale(o_ref[...]) * fg_ref[...]


def _peer_experts(xn, gate, u_tbl, v_tbl, h, final_gain, final_norm):
    n, d = h.shape
    n_exp = u_tbl.shape[0]
    t = min(EXPERT_TOKENS, n)
    assert n % t == 0 and n_exp == PEER_N_KEYS * PEER_N_KEYS
    gate2 = gate.reshape(gate.shape[0], n * EXPERT_ROWS, PEER_N_KEYS)
    return pl.pallas_call(
        functools.partial(_peer_expert_kernel, final_norm=final_norm),
        grid=(n // t, n_exp // EXPERT_BLOCK),
        in_specs=[
            pl.BlockSpec((t, d), lambda i, e: (i, 0)),
            pl.BlockSpec((1, t * EXPERT_ROWS, PEER_N_KEYS), lambda i, e: (e, i, 0)),
            pl.BlockSpec((EXPERT_BLOCK, d), lambda i, e: (e, 0)),
            pl.BlockSpec((EXPERT_BLOCK, d), lambda i, e: (e, 0)),
            pl.BlockSpec((t, d), lambda i, e: (i, 0)),
            pl.BlockSpec((1, d), lambda i, e: (0, 0)),
        ],
        out_specs=pl.BlockSpec((t, d), lambda i, e: (i, 0)),
        out_shape=jax.ShapeDtypeStruct((n, d), F32),
        scratch_shapes=[pltpu.VMEM((t, EXPERT_BLOCK), BF16)],
        compiler_params=_params("arbitrary", "arbitrary"),
        name="peer_experts",
    )(xn, gate2, u_tbl.astype(BF16), v_tbl.astype(BF16), h, final_gain.reshape(1, d))


def _peer_ffn(h, gain, w_query, subkeys, u_tbl, v_tbl, final_gain, final_norm):
    xn, gate = _peer_route(h, gain, w_query, subkeys)
    return _peer_experts(xn, gate, u_tbl, v_tbl, h, final_gain, final_norm)


def _kvq_kernel(h_ref, hp_ref, gkv_ref, gq_ref, wk_ref, wvt_ref, wqt_ref,
                k_ref, vt_ref, qt_ref):
    n_heads = qt_ref.shape[1]
    t = h_ref.shape[0]
    xkv = (_rms_scale(hp_ref[...]) * gkv_ref[...]).astype(BF16)
    xq = (_rms_scale(h_ref[...]) * gq_ref[...]).astype(BF16)
    k_ref[...] = jnp.dot(xkv, wk_ref[...], preferred_element_type=F32).astype(BF16)
    vt_ref[0] = lax.dot_general(wvt_ref[...], xkv, _NT,
                                preferred_element_type=F32).astype(BF16)
    qt = (lax.dot_general(wqt_ref[...], xq, _NT, preferred_element_type=F32)
          * (HEAD_DIM ** -0.5)).astype(BF16)
    zeros = jnp.zeros((HEAD_DIM, t), BF16)
    for hd in range(n_heads):
        q_h = qt[hd * HEAD_DIM:(hd + 1) * HEAD_DIM, :]
        if hd % 2 == 0:
            qt_ref[0, hd, 0:HEAD_DIM, :] = q_h
            qt_ref[0, hd, HEAD_DIM:2 * HEAD_DIM, :] = zeros
        else:
            qt_ref[0, hd, 0:HEAD_DIM, :] = zeros
            qt_ref[0, hd, HEAD_DIM:2 * HEAD_DIM, :] = q_h


def _key_permute(x, seq_len):
    n, d = x.shape
    kb = min(ATTN_BLOCK, seq_len)
    seg = kb // SUBLANES
    return x.reshape(n // kb, SUBLANES, seg, d).swapaxes(1, 2).reshape(n, d)


def _kvq_proj(h, g_kv, g_q, w_kv, w_q, batch, seq_len):
    n, d = h.shape
    n_heads = d // HEAD_DIM
    t = min(PROJ_TOKENS, seq_len)
    tiles = seq_len // t
    assert seq_len % t == 0 and t % min(ATTN_BLOCK, seq_len) == 0 and n_heads % 2 == 0
    w_k, w_v = w_kv[:, :d], w_kv[:, d:]
    return pl.pallas_call(
        _kvq_kernel,
        grid=(n // t,),
        in_specs=[
            pl.BlockSpec((t, d), lambda i: (i, 0)),
            pl.BlockSpec((t, d), lambda i: (i, 0)),
            pl.BlockSpec((1, d), lambda i: (0, 0)),
            pl.BlockSpec((1, d), lambda i: (0, 0)),
            pl.BlockSpec((d, d), lambda i: (0, 0)),
            pl.BlockSpec((d, d), lambda i: (0, 0)),
            pl.BlockSpec((d, d), lambda i: (0, 0)),
        ],
        out_specs=[
            pl.BlockSpec((t, d), lambda i: (i, 0)),
            pl.BlockSpec((1, d, t), lambda i: (i // tiles, 0, i % tiles)),
            pl.BlockSpec((1, n_heads, 2 * HEAD_DIM, t), lambda i: (i // tiles, 0, 0, i % tiles)),
        ],
        out_shape=[
            jax.ShapeDtypeStruct((n, d), BF16),
            jax.ShapeDtypeStruct((batch, d, seq_len), BF16),
            jax.ShapeDtypeStruct((batch, n_heads, 2 * HEAD_DIM, seq_len), BF16),
        ],
        compiler_params=_params("arbitrary"),
        name="kvq_proj",
    )(h, _key_permute(h, seq_len), g_kv.reshape(1, d), g_q.reshape(1, d),
      w_k.astype(BF16), w_v.T.astype(BF16), w_q.T.astype(BF16))


def _attn_kernel(qt_ref, k_ref, vt_ref, o_ref, *, blk):
    seg = blk // SUBLANES
    width = 2 * blk
    qi = pl.program_id(2)
    qt = jnp.concatenate([qt_ref[0, 0], qt_ref[0, 1]], axis=1)

    def block(kb, carry, acc0, acc1, masked):
        k0 = pl.multiple_of(kb * blk, blk)
        z = jnp.dot(k_ref[0, pl.ds(k0, blk), :], qt, preferred_element_type=F32)
        if masked:
            row = lax.broadcasted_iota(jnp.int32, (blk, width), 0)
            lane = lax.broadcasted_iota(jnp.int32, (blk, width), 1)
            key_local = (row % SUBLANES) * seg + row // SUBLANES
            z = jnp.where(key_local < lane % blk, z, -1e30)
        e = jnp.exp(-jnp.abs(z))
        r = 1.0 / (1.0 + e)
        er = e * r
        pos = z >= 0.0
        beta = jnp.where(pos, r, er)
        keep = jnp.where(pos, er, r)
        run = jnp.ones((SUBLANES, width), F32)
        parts = [None] * seg
        for rr in reversed(range(seg)):
            parts[rr] = run
            run = run * keep[rr * SUBLANES:(rr + 1) * SUBLANES, :]
        offs = [None] * SUBLANES
        c = carry
        for s in reversed(range(SUBLANES)):
            offs[s] = c
            c = c * run[s:s + 1, :]
        off = jnp.concatenate(offs, axis=0)
        w = jnp.concatenate(
            [beta[rr * SUBLANES:(rr + 1) * SUBLANES, :] * (parts[rr] * off)
             for rr in range(seg)], axis=0).astype(BF16)
        vt = vt_ref[0, :, pl.ds(k0, blk)]
        acc0 = acc0 + jnp.dot(vt[0:HEAD_DIM, :], w[:, 0:blk], preferred_element_type=F32)
        acc1 = acc1 + jnp.dot(vt[HEAD_DIM:, :], w[:, blk:], preferred_element_type=F32)
        return c, acc0, acc1

    carry = jnp.ones((1, width), F32)
    zero = jnp.zeros((HEAD_DIM, blk), F32)
    state = (jnp.int32(0),) + block(qi, carry, zero, zero, True)

    def cond(st):
        return jnp.logical_and(st[0] < qi, jnp.max(st[1]) > 0.0)

    def body(st):
        return (st[0] + 1,) + block(qi - 1 - st[0], st[1], st[2], st[3], False)

    state = lax.while_loop(cond, body, state)
    o_ref[0, 0:HEAD_DIM, :] = state[2]
    o_ref[0, HEAD_DIM:, :] = state[3]


def _stick_breaking_attention(qt_pad, k_perm, vt_perm, seq_len):
    batch, n_heads = qt_pad.shape[0], qt_pad.shape[1]
    d = n_heads * HEAD_DIM
    blk = min(ATTN_BLOCK, seq_len)
    k3 = k_perm.reshape(batch, seq_len, d)
    return pl.pallas_call(
        functools.partial(_attn_kernel, blk=blk),
        grid=(batch, n_heads // 2, seq_len // blk),
        in_specs=[
            pl.BlockSpec((1, 2, 2 * HEAD_DIM, blk), lambda b, p, q: (b, p, 0, q)),
            pl.BlockSpec((1, seq_len, 2 * HEAD_DIM), lambda b, p, q: (b, 0, p)),
            pl.BlockSpec((1, 2 * HEAD_DIM, seq_len), lambda b, p, q: (b, p, 0)),
        ],
        out_specs=pl.BlockSpec((1, 2 * HEAD_DIM, blk), lambda b, p, q: (b, p, q)),
        out_shape=jax.ShapeDtypeStruct((batch, d, seq_len), F32),
        compiler_params=_params("arbitrary", "arbitrary", "arbitrary"),
        name="stick_breaking_attention",
    )(qt_pad, k3, vt_perm)


def _out_proj_kernel(ot_ref, w_ref, h_ref, o_ref):
    a = ot_ref[0].T.astype(BF16)
    o_ref[...] = h_ref[...] + jnp.dot(a, w_ref[...], preferred_element_type=F32)


def _out_proj(ot, w_o, h, seq_len):
    n, d = h.shape
    t = min(PROJ_TOKENS, seq_len)
    tiles = seq_len // t
    return pl.pallas_call(
        _out_proj_kernel,
        grid=(n // t,),
        in_specs=[
            pl.BlockSpec((1, d, t), lambda i: (i // tiles, 0, i % tiles)),
            pl.BlockSpec((d, d), lambda i: (0, 0)),
            pl.BlockSpec((t, d), lambda i: (i, 0)),
        ],
        out_specs=pl.BlockSpec((t, d), lambda i: (i, 0)),
        out_shape=jax.ShapeDtypeStruct((n, d), F32),
        compiler_params=_params("arbitrary"),
        name="attn_out_proj",
    )(ot, w_o.astype(BF16), h)


def kernel(x, norm_mix, norm_ffn, conv_w_in, conv_kernel, conv_w_out, norm_kv, w_kv,
           attn_w_q, attn_w_o, peer_w_query, peer_subkeys, peer_u, peer_v, norm_final):
    batch, seq_len, d = x.shape
    assert norm_mix.shape[0] == 2 and conv_w_in.shape[0] == 1 and attn_w_q.shape[0] == 1
    h = x.reshape(batch * seq_len, d)
    h = _conv_mixer(h, norm_mix[0], conv_w_in[0], conv_kernel[0], conv_w_out[0], seq_len)
    h = _peer_ffn(h, norm_ffn[0], peer_w_query[0], peer_subkeys[0], peer_u[0], peer_v[0],
                  norm_final, final_norm=False)
    k_perm, vt_perm, qt_pad = _kvq_proj(h, norm_kv, norm_mix[1], w_kv, attn_w_q[0],
                                        batch, seq_len)
    ot = _stick_breaking_attention(qt_pad, k_perm, vt_perm, seq_len)
    h = _out_proj(ot, attn_w_o[0], h, seq_len)
    h = _peer_ffn(h, norm_ffn[1], peer_w_query[1], peer_subkeys[1], peer_u[1], peer_v[1],
                  norm_final, final_norm=True)
    return h.reshape(batch, seq_len, d)
```

```python
import functools
import math

import jax
import jax.numpy as jnp
from jax import lax
from jax.experimental import pallas as pl
from jax.experimental.pallas import tpu as pltpu

F32 = jnp.float32
BF16 = jnp.bfloat16

RMS_EPS = 1e-6
CONV_WIDTH = 3
HEAD_DIM = 64
PEER_HEADS = 8
PEER_N_KEYS = 128
PEER_KEY_DIM = 128
PEER_TOPK = 16

SUBLANES = 8
LANES = 128
VMEM_LIMIT_BYTES = 56 * 1024 * 1024

CONV_TOKENS = 512
TOPK_TOKENS = SUBLANES * LANES
SCORE_PITCH = PEER_N_KEYS + SUBLANES
TOPK_CHAINS = 4
GATE_TOKENS = LANES
GATE_UNROLL = 32
EXPERT_TOKENS = 1024
EXPERT_ROWS = SUBLANES
EXPERT_BLOCK = EXPERT_ROWS * PEER_N_KEYS
N_EXPERT_BLOCKS = PEER_N_KEYS // EXPERT_ROWS
EXPERT_CHUNK = 2 * PEER_N_KEYS
PROJ_TOKENS = 512
ATTN_BLOCK = 256

_NT = (((1,), (1,)), ((), ()))


def _params(*sem):
    return pltpu.CompilerParams(dimension_semantics=sem,
                                vmem_limit_bytes=VMEM_LIMIT_BYTES)


def _rms_scale(x):
    return x * lax.rsqrt(jnp.mean(x * x, axis=-1, keepdims=True) + RMS_EPS)


def _conv_mixer_kernel(x_ref, g_ref, win_ref, ck_ref, wout_ref, o_ref, ubuf_ref,
                       *, tiles_per_seq):
    t = x_ref.shape[0]
    d = x_ref.shape[1]
    halo = SUBLANES
    first = (pl.program_id(0) % tiles_per_seq) == 0

    @pl.when(first)
    def _():
        ubuf_ref[0:halo, :] = jnp.zeros((halo, d), F32)

    @pl.when(jnp.logical_not(first))
    def _():
        ubuf_ref[0:halo, :] = ubuf_ref[t:t + halo, :]

    x = x_ref[...]
    xb = (_rms_scale(x) * g_ref[...]).astype(BF16)
    bg = jnp.dot(xb, win_ref[:, 0:d], preferred_element_type=F32)
    cg = jnp.dot(xb, win_ref[:, d:2 * d], preferred_element_type=F32)
    hv = jnp.dot(xb, win_ref[:, 2 * d:3 * d], preferred_element_type=F32)
    u = cg * hv
    ubuf_ref[halo:halo + t, :] = u
    conv = ubuf_ref[halo - 2:halo - 2 + t, :] * ck_ref[0:1, :]
    conv = conv + ubuf_ref[halo - 1:halo - 1 + t, :] * ck_ref[1:2, :]
    conv = conv + u * ck_ref[2:3, :]
    y = (bg * conv).astype(BF16)
    o_ref[...] = x + jnp.dot(y, wout_ref[...], preferred_element_type=F32)


def _conv_mixer(h, gain, w_in, conv_k, w_out, seq_len):
    n, d = h.shape
    t = min(CONV_TOKENS, seq_len)
    assert seq_len % t == 0 and conv_k.shape[0] == CONV_WIDTH
    return pl.pallas_call(
        functools.partial(_conv_mixer_kernel, tiles_per_seq=seq_len // t),
        grid=(n // t,),
        in_specs=[
            pl.BlockSpec((t, d), lambda i: (i, 0)),
            pl.BlockSpec((1, d), lambda i: (0, 0)),
            pl.BlockSpec((d, 3 * d), lambda i: (0, 0)),
            pl.BlockSpec((CONV_WIDTH, d), lambda i: (0, 0)),
            pl.BlockSpec((d, d), lambda i: (0, 0)),
        ],
        out_specs=pl.BlockSpec((t, d), lambda i: (i, 0)),
        out_shape=jax.ShapeDtypeStruct((n, d), F32),
        scratch_shapes=[pltpu.VMEM((t + 2 * SUBLANES, d), F32)],
        compiler_params=_params("arbitrary"),
        name="conv_mixer",
    )(h, gain.reshape(1, d), w_in.astype(BF16), conv_k, w_out.astype(BF16))


def _top_planes(planes, ids, k):
    planes = list(planes)
    chunk = -(-len(planes) // TOPK_CHAINS)
    vals, picked = [], []
    for it in range(k):
        best = best_id = None
        for lo in range(0, len(planes), chunk):
            part = planes[lo]
            part_id = jnp.full(part.shape, ids[lo], F32)
            for plane, plane_id in zip(planes[lo + 1:lo + chunk], ids[lo + 1:lo + chunk]):
                better = plane > part
                part = jnp.where(better, plane, part)
                part_id = jnp.where(better, plane_id, part_id)
            if best is None:
                best, best_id = part, part_id
            else:
                better = part > best
                best = jnp.where(better, part, best)
                best_id = jnp.where(better, part_id, best_id)
        vals.append(best)
        picked.append(best_id)
        if it + 1 < k:
            planes = [jnp.where(best_id == plane_id, -jnp.inf, plane)
                      for plane, plane_id in zip(planes, ids)]
    return vals, picked


def _lookup(table, sel):
    out = jnp.zeros(sel.shape, F32)
    for a, plane in enumerate(table):
        out = out + jnp.where(sel == float(a), plane, 0.0)
    return out


def _score_planes(scores, scratch_ref):
    for g in range(SUBLANES):
        scratch_ref[g * SCORE_PITCH:g * SCORE_PITCH + PEER_N_KEYS, :] = (
            scores[:, g * LANES:(g + 1) * LANES])
    return [scratch_ref[pl.ds(n, SUBLANES, stride=SCORE_PITCH), :] for n in range(PEER_N_KEYS)]


def _peer_topk_kernel(h_ref, g_ref, wqt_ref, sk_ref, xn_ref, picks_ref, s0_ref, s1_ref):
    k = PEER_TOPK
    half = k // 2
    xn_ref[...] = (_rms_scale(h_ref[...]) * g_ref[...]).astype(BF16)
    key_ids = [float(n) for n in range(PEER_N_KEYS)]
    cand_ab = ([(0, b) for b in range(k)]
               + [(a, b) for a in range(1, half) for b in range(half)]
               + [(a, 0) for a in range(half, k)])
    cand_ids = [float(a * k + b) for a, b in cand_ab]

    def head_body(hd, carry):
        q0 = pl.multiple_of(hd * 2 * PEER_KEY_DIM, 2 * PEER_KEY_DIM)
        qt = lax.dot_general(wqt_ref[pl.ds(q0, 2 * PEER_KEY_DIM), :], xn_ref[...], _NT,
                             preferred_element_type=F32).astype(BF16)
        tops = []
        for p, scratch_ref in enumerate((s0_ref, s1_ref)):
            scores = jnp.dot(sk_ref[hd, p], qt[p * PEER_KEY_DIM:(p + 1) * PEER_KEY_DIM, :],
                             preferred_element_type=F32)
            tops.append(_top_planes(_score_planes(scores, scratch_ref), key_ids, k))
        (v1, i1), (v2, i2) = tops
        top_s, top_c = _top_planes([v1[a] + v2[b] for a, b in cand_ab], cand_ids, k)
        denom = None
        gates = []
        for j in range(k):
            gates.append(jnp.exp(top_s[j] - top_s[0]))
            denom = gates[j] if denom is None else denom + gates[j]
        for j in range(k):
            a_sel = jnp.floor(top_c[j] * (1.0 / k))
            b_sel = top_c[j] - a_sel * k
            row0 = pl.multiple_of((hd * k + j) * SUBLANES, SUBLANES)
            picks_ref[0, 0, pl.ds(row0, SUBLANES), :] = _lookup(i1, a_sel)
            picks_ref[0, 1, pl.ds(row0, SUBLANES), :] = _lookup(i2, b_sel)
            picks_ref[0, 2, pl.ds(row0, SUBLANES), :] = gates[j] / denom
        return carry

    lax.fori_loop(0, PEER_HEADS, head_body, 0)


def _peer_topk(h, gain, w_query, subkeys):
    n, d = h.shape
    t = TOPK_TOKENS
    qdim = PEER_HEADS * 2 * PEER_KEY_DIM
    n_picks = PEER_HEADS * PEER_TOPK
    assert subkeys.shape == (PEER_HEADS, 2, PEER_N_KEYS, PEER_KEY_DIM)
    assert w_query.shape == (d, qdim) and n % t == 0
    return pl.pallas_call(
        _peer_topk_kernel,
        grid=(n // t,),
        in_specs=[
            pl.BlockSpec((t, d), lambda i: (i, 0)),
            pl.BlockSpec((1, d), lambda i: (0, 0)),
            pl.BlockSpec((qdim, d), lambda i: (0, 0)),
            pl.BlockSpec((PEER_HEADS, 2, PEER_N_KEYS, PEER_KEY_DIM), lambda i: (0, 0, 0, 0)),
        ],
        out_specs=[
            pl.BlockSpec((t, d), lambda i: (i, 0)),
            pl.BlockSpec((1, 3, n_picks * SUBLANES, LANES), lambda i: (i, 0, 0, 0)),
        ],
        out_shape=[
            jax.ShapeDtypeStruct((n, d), BF16),
            jax.ShapeDtypeStruct((n // t, 3, n_picks * SUBLANES, LANES), F32),
        ],
        scratch_shapes=[pltpu.VMEM((SUBLANES * SCORE_PITCH, LANES), F32)] * 2,
        compiler_params=_params("arbitrary"),
        name="peer_topk",
    )(h, gain.reshape(1, d), w_query.T.astype(BF16), subkeys.astype(BF16))


def _gate_build_kernel(picks_ref, gate_ref, i1t_s, i2t_s, gtt_s):
    n_picks = picks_ref.shape[2] // SUBLANES
    group = pl.program_id(1)
    for which, dst in enumerate((i1t_s, i2t_s, gtt_s)):
        dst[...] = picks_ref[0, which, pl.ds(group, n_picks, stride=SUBLANES), :].T
    sub_id = lax.broadcasted_iota(jnp.int32, (PEER_N_KEYS, n_picks), 0).astype(F32)

    def token_body(c, carry):
        i1r = i1t_s[pl.ds(c, 1), :]
        i2r = i2t_s[pl.ds(c, 1), :]
        gr = gtt_s[pl.ds(c, 1), :]
        a = jnp.where(sub_id == i1r, gr, 0.0).astype(BF16)
        b = jnp.where(sub_id == i2r, 1.0, 0.0).astype(BF16)
        g = lax.dot_general(a, b, _NT, preferred_element_type=F32)
        gate_ref[:, c, :, :] = g.reshape(N_EXPERT_BLOCKS, EXPERT_ROWS, PEER_N_KEYS)
        return carry

    lax.fori_loop(0, GATE_TOKENS, token_body, 0, unroll=GATE_UNROLL)


def _gate_build(picks):
    tiles, _, pick_rows, _ = picks.shape
    n = tiles * TOPK_TOKENS
    n_picks = pick_rows // SUBLANES
    return pl.pallas_call(
        _gate_build_kernel,
        grid=(tiles, SUBLANES),
        in_specs=[pl.BlockSpec((1, 3, pick_rows, LANES), lambda i, g: (i, 0, 0, 0))],
        out_specs=pl.BlockSpec((N_EXPERT_BLOCKS, GATE_TOKENS, EXPERT_ROWS, PEER_N_KEYS),
                               lambda i, g: (0, i * SUBLANES + g, 0, 0)),
        out_shape=jax.ShapeDtypeStruct((N_EXPERT_BLOCKS, n, EXPERT_ROWS, PEER_N_KEYS), F32),
        scratch_shapes=[pltpu.VMEM((GATE_TOKENS, n_picks), F32)] * 3,
        compiler_params=_params("arbitrary", "arbitrary"),
        name="peer_gate_build",
    )(picks)


def _peer_expert_kernel(xn_ref, gate_ref, u_ref, v_ref, h_ref, fg_ref, o_ref, coef_ref,
                        *, final_norm):
    e = pl.program_id(1)
    t = xn_ref.shape[0]

    @pl.when(e == 0)
    def _():
        o_ref[...] = h_ref[...]

    xn = xn_ref[...]
    per_chunk = EXPERT_CHUNK // PEER_N_KEYS
    for q in range(EXPERT_BLOCK // EXPERT_CHUNK):
        cols = slice(q * EXPERT_CHUNK, (q + 1) * EXPERT_CHUNK)
        act = lax.dot_general(xn, u_ref[cols, :], _NT, preferred_element_type=F32)
        gelu = 0.5 * act * (1.0 + lax.erf(act * (1.0 / math.sqrt(2.0))))
        gate = jnp.concatenate(
            [gate_ref[0, pl.ds(q * per_chunk + j, t, stride=EXPERT_ROWS), :]
             for j in range(per_chunk)], axis=1)
        coef_ref[:, cols] = (gate * gelu).astype(BF16)
    o_ref[...] += jnp.dot(coef_ref[...], v_ref[...], preferred_element_type=F32)

    if final_norm:
        @pl.when(e == pl.num_programs(1) - 1)
        def _():
            o_ref[...] = _rms_scale(o_ref[...]) * fg_ref[...]


def _peer_experts(xn, gate, u_tbl, v_tbl, h, final_gain, final_norm):
    n, d = h.shape
    t = min(EXPERT_TOKENS, n)
    assert n % t == 0 and u_tbl.shape == (N_EXPERT_BLOCKS * EXPERT_BLOCK, d)
    gate2 = gate.reshape(gate.shape[0], n * EXPERT_ROWS, PEER_N_KEYS)
    return pl.pallas_call(
        functools.partial(_peer_expert_kernel, final_norm=final_norm),
        grid=(n // t, N_EXPERT_BLOCKS),
        in_specs=[
            pl.BlockSpec((t, d), lambda i, e: (i, 0)),
            pl.BlockSpec((1, t * EXPERT_ROWS, PEER_N_KEYS), lambda i, e: (e, i, 0)),
            pl.BlockSpec((EXPERT_BLOCK, d), lambda i, e: (e, 0)),
            pl.BlockSpec((EXPERT_BLOCK, d), lambda i, e: (e, 0)),
            pl.BlockSpec((t, d), lambda i, e: (i, 0)),
            pl.BlockSpec((1, d), lambda i, e: (0, 0)),
        ],
        out_specs=pl.BlockSpec((t, d), lambda i, e: (i, 0)),
        out_shape=jax.ShapeDtypeStruct((n, d), F32),
        scratch_shapes=[pltpu.VMEM((t, EXPERT_BLOCK), BF16)],
        compiler_params=_params("arbitrary", "arbitrary"),
        name="peer_experts",
    )(xn, gate2, u_tbl.astype(BF16), v_tbl.astype(BF16), h, final_gain.reshape(1, d))


def _peer_ffn(h, gain, w_query, subkeys, u_tbl, v_tbl, final_gain, final_norm):
    xn, picks = _peer_topk(h, gain, w_query, subkeys)
    return _peer_experts(xn, _gate_build(picks), u_tbl, v_tbl, h, final_gain, final_norm)


def _kvq_kernel(h_ref, hp_ref, gkv_ref, gq_ref, wk_ref, wvt_ref, wqt_ref,
                k_ref, vt_ref, qt_ref):
    n_heads = qt_ref.shape[1]
    t = h_ref.shape[0]
    xkv = (_rms_scale(hp_ref[...]) * gkv_ref[...]).astype(BF16)
    xq = (_rms_scale(h_ref[...]) * gq_ref[...]).astype(BF16)
    k_ref[...] = jnp.dot(xkv, wk_ref[...], preferred_element_type=F32).astype(BF16)
    vt_ref[0] = lax.dot_general(wvt_ref[...], xkv, _NT,
                                preferred_element_type=F32).astype(BF16)
    qt = (lax.dot_general(wqt_ref[...], xq, _NT, preferred_element_type=F32)
          * (0.5 * HEAD_DIM ** -0.5)).astype(BF16)
    zeros = jnp.zeros((HEAD_DIM, t), BF16)
    for hd in range(n_heads):
        q_h = qt[hd * HEAD_DIM:(hd + 1) * HEAD_DIM, :]
        if hd % 2 == 0:
            qt_ref[0, hd, 0:HEAD_DIM, :] = q_h
            qt_ref[0, hd, HEAD_DIM:2 * HEAD_DIM, :] = zeros
        else:
            qt_ref[0, hd, 0:HEAD_DIM, :] = zeros
            qt_ref[0, hd, HEAD_DIM:2 * HEAD_DIM, :] = q_h


def _key_permute(x, seq_len):
    n, d = x.shape
    kb = min(ATTN_BLOCK, seq_len)
    seg = kb // SUBLANES
    return x.reshape(n // kb, SUBLANES, seg, d).swapaxes(1, 2).reshape(n, d)


def _kvq_proj(h, g_kv, g_q, w_kv, w_q, batch, seq_len):
    n, d = h.shape
    n_heads = d // HEAD_DIM
    t = min(PROJ_TOKENS, seq_len)
    tiles = seq_len // t
    assert seq_len % t == 0 and t % min(ATTN_BLOCK, seq_len) == 0 and n_heads % 2 == 0
    w_k, w_v = w_kv[:, :d], w_kv[:, d:]
    return pl.pallas_call(
        _kvq_kernel,
        grid=(n // t,),
        in_specs=[
            pl.BlockSpec((t, d), lambda i: (i, 0)),
            pl.BlockSpec((t, d), lambda i: (i, 0)),
            pl.BlockSpec((1, d), lambda i: (0, 0)),
            pl.BlockSpec((1, d), lambda i: (0, 0)),
            pl.BlockSpec((d, d), lambda i: (0, 0)),
            pl.BlockSpec((d, d), lambda i: (0, 0)),
            pl.BlockSpec((d, d), lambda i: (0, 0)),
        ],
        out_specs=[
            pl.BlockSpec((t, d), lambda i: (i, 0)),
            pl.BlockSpec((1, d, t), lambda i: (i // tiles, 0, i % tiles)),
            pl.BlockSpec((1, n_heads, 2 * HEAD_DIM, t), lambda i: (i // tiles, 0, 0, i % tiles)),
        ],
        out_shape=[
            jax.ShapeDtypeStruct((n, d), BF16),
            jax.ShapeDtypeStruct((batch, d, seq_len), BF16),
            jax.ShapeDtypeStruct((batch, n_heads, 2 * HEAD_DIM, seq_len), BF16),
        ],
        compiler_params=_params("arbitrary"),
        name="kvq_proj",
    )(h, _key_permute(h, seq_len), g_kv.reshape(1, d), g_q.reshape(1, d),
      w_k.astype(BF16), w_v.T.astype(BF16), w_q.T.astype(BF16))


def _attn_kernel(qt_ref, k_ref, vt_ref, bias_ref, o_ref, *, blk):
    seg = blk // SUBLANES
    width = 2 * blk
    qi = pl.program_id(2)
    qt = jnp.concatenate([qt_ref[0, 0], qt_ref[0, 1]], axis=1)

    def block(kb, carry, acc0, acc1, masked):
        k0 = pl.multiple_of(kb * blk, blk)
        zh = jnp.dot(k_ref[0, pl.ds(k0, blk), :], qt, preferred_element_type=F32)
        if masked:
            zh = zh + bias_ref[...]
        ht = 0.5 * jnp.tanh(zh)
        beta = 0.5 + ht
        keep = 0.5 - ht
        run = jnp.ones((SUBLANES, width), F32)
        parts = [None] * seg
        for rr in reversed(range(seg)):
            parts[rr] = run
            run = run * keep[rr * SUBLANES:(rr + 1) * SUBLANES, :]
        offs = [None] * SUBLANES
        c = carry
        for s in reversed(range(SUBLANES)):
            offs[s] = c
            c = c * run[s:s + 1, :]
        off = jnp.concatenate(offs, axis=0)
        w = jnp.concatenate(
            [beta[rr * SUBLANES:(rr + 1) * SUBLANES, :] * (parts[rr] * off)
             for rr in range(seg)], axis=0).astype(BF16)
        vt = vt_ref[0, :, pl.ds(k0, blk)]
        acc0 = acc0 + jnp.dot(vt[0:HEAD_DIM, :], w[:, 0:blk], preferred_element_type=F32)
        acc1 = acc1 + jnp.dot(vt[HEAD_DIM:, :], w[:, blk:], preferred_element_type=F32)
        return c, acc0, acc1

    carry = jnp.ones((1, width), F32)
    zero = jnp.zeros((HEAD_DIM, blk), F32)
    state = (jnp.int32(0),) + block(qi, carry, zero, zero, True)

    def cond(st):
        return jnp.logical_and(st[0] < qi, jnp.max(st[1]) > 0.0)

    def body(st):
        return (st[0] + 1,) + block(qi - 1 - st[0], st[1], st[2], st[3], False)

    state = lax.while_loop(cond, body, state)
    o_ref[0, 0:HEAD_DIM, :] = state[2]
    o_ref[0, HEAD_DIM:, :] = state[3]


def _causal_bias(blk):
    seg = blk // SUBLANES
    row = jnp.arange(blk, dtype=jnp.int32)[:, None]
    key_local = (row % SUBLANES) * seg + row // SUBLANES
    query_local = jnp.arange(2 * blk, dtype=jnp.int32)[None, :] % blk
    return jnp.where(key_local < query_local, 0.0, -1e30).astype(F32)


def _stick_breaking_attention(qt_pad, k_perm, vt_perm, seq_len):
    batch, n_heads = qt_pad.shape[0], qt_pad.shape[1]
    d = n_heads * HEAD_DIM
    blk = min(ATTN_BLOCK, seq_len)
    k3 = k_perm.reshape(batch, seq_len, d)
    return pl.pallas_call(
        functools.partial(_attn_kernel, blk=blk),
        grid=(batch, n_heads // 2, seq_len // blk),
        in_specs=[
            pl.BlockSpec((1, 2, 2 * HEAD_DIM, blk), lambda b, p, q: (b, p, 0, q)),
            pl.BlockSpec((1, seq_len, 2 * HEAD_DIM), lambda b, p, q: (b, 0, p)),
            pl.BlockSpec((1, 2 * HEAD_DIM, seq_len), lambda b, p, q: (b, p, 0)),
            pl.BlockSpec((blk, 2 * blk), lambda b, p, q: (0, 0)),
        ],
        out_specs=pl.BlockSpec((1, 2 * HEAD_DIM, blk), lambda b, p, q: (b, p, q)),
        out_shape=jax.ShapeDtypeStruct((batch, d, seq_len), F32),
        compiler_params=_params("arbitrary", "arbitrary", "arbitrary"),
        name="stick_breaking_attention",
    )(qt_pad, k3, vt_perm, _causal_bias(blk))


def _out_proj_kernel(ot_ref, w_ref, h_ref, o_ref):
    a = ot_ref[0].T.astype(BF16)
    o_ref[...] = h_ref[...] + jnp.dot(a, w_ref[...], preferred_element_type=F32)


def _out_proj(ot, w_o, h, seq_len):
    n, d = h.shape
    t = min(PROJ_TOKENS, seq_len)
    tiles = seq_len // t
    return pl.pallas_call(
        _out_proj_kernel,
        grid=(n // t,),
        in_specs=[
            pl.BlockSpec((1, d, t), lambda i: (i // tiles, 0, i % tiles)),
            pl.BlockSpec((d, d), lambda i: (0, 0)),
            pl.BlockSpec((t, d), lambda i: (i, 0)),
        ],
        out_specs=pl.BlockSpec((t, d), lambda i: (i, 0)),
        out_shape=jax.ShapeDtypeStruct((n, d), F32),
        compiler_params=_params("arbitrary"),
        name="attn_out_proj",
    )(ot, w_o.astype(BF16), h)


def kernel(x, norm_mix, norm_ffn, conv_w_in, conv_kernel, conv_w_out, norm_kv, w_kv,
           attn_w_q, attn_w_o, peer_w_query, peer_subkeys, peer_u, peer_v, norm_final):
    batch, seq_len, d = x.shape
    assert norm_mix.shape[0] == 2 and conv_w_in.shape[0] == 1 and attn_w_q.shape[0] == 1
    h = x.reshape(batch * seq_len, d)
    h = _conv_mixer(h, norm_mix[0], conv_w_in[0], conv_kernel[0], conv_w_out[0], seq_len)
    h = _peer_ffn(h, norm_ffn[0], peer_w_query[0], peer_subkeys[0], peer_u[0], peer_v[0],
                  norm_final, final_norm=False)
    k_perm, vt_perm, qt_pad = _kvq_proj(h, norm_kv, norm_mix[1], w_kv, attn_w_q[0],
                                        batch, seq_len)
    ot = _stick_breaking_attention(qt_pad, k_perm, vt_perm, seq_len)
    h = _out_proj(ot, attn_w_o[0], h, seq_len)
    h = _peer_ffn(h, norm_ffn[1], peer_w_query[1], peer_subkeys[1], peer_u[1], peer_v[1],
                  norm_final, final_norm=True)
    return h.reshape(batch, seq_len, d)
```

```python
import functools
import math

import jax
import jax.numpy as jnp
from jax import lax
from jax.experimental import pallas as pl
from jax.experimental.pallas import tpu as pltpu

F32 = jnp.float32
BF16 = jnp.bfloat16

RMS_EPS = 1e-6
CONV_WIDTH = 3
HEAD_DIM = 64
PEER_HEADS = 8
PEER_N_KEYS = 128
PEER_KEY_DIM = 128
PEER_TOPK = 16

SUBLANES = 8
LANES = 128
VMEM_LIMIT_BYTES = 56 * 1024 * 1024

CONV_TOKENS = 512
TOPK_TOKENS = SUBLANES * LANES
SCORE_PITCH = PEER_N_KEYS + SUBLANES
TOPK_CHAINS = 4
GATE_TOKENS = LANES
GATE_UNROLL = 32
EXPERT_TOKENS = 1024
EXPERT_ROWS = 2 * SUBLANES
EXPERT_BLOCK = EXPERT_ROWS * PEER_N_KEYS
N_EXPERT_BLOCKS = PEER_N_KEYS // EXPERT_ROWS
EXPERT_CHUNK = 2 * PEER_N_KEYS
PROJ_TOKENS = 512
ATTN_BLOCK = 256
ATTN_TILES = 2

_NT = (((1,), (1,)), ((), ()))


def _params(*sem):
    return pltpu.CompilerParams(dimension_semantics=sem,
                                vmem_limit_bytes=VMEM_LIMIT_BYTES)


def _rms_scale(x):
    return x * lax.rsqrt(jnp.mean(x * x, axis=-1, keepdims=True) + RMS_EPS)


def _conv_mixer_kernel(x_ref, g_ref, win_ref, ck_ref, wout_ref, o_ref, ubuf_ref,
                       *, tiles_per_seq):
    t = x_ref.shape[0]
    d = x_ref.shape[1]
    halo = SUBLANES
    first = (pl.program_id(0) % tiles_per_seq) == 0

    @pl.when(first)
    def _():
        ubuf_ref[0:halo, :] = jnp.zeros((halo, d), F32)

    @pl.when(jnp.logical_not(first))
    def _():
        ubuf_ref[0:halo, :] = ubuf_ref[t:t + halo, :]

    x = x_ref[...]
    xb = (_rms_scale(x) * g_ref[...]).astype(BF16)
    bg = jnp.dot(xb, win_ref[:, 0:d], preferred_element_type=F32)
    cg = jnp.dot(xb, win_ref[:, d:2 * d], preferred_element_type=F32)
    hv = jnp.dot(xb, win_ref[:, 2 * d:3 * d], preferred_element_type=F32)
    u = cg * hv
    ubuf_ref[halo:halo + t, :] = u
    conv = ubuf_ref[halo - 2:halo - 2 + t, :] * ck_ref[0:1, :]
    conv = conv + ubuf_ref[halo - 1:halo - 1 + t, :] * ck_ref[1:2, :]
    conv = conv + u * ck_ref[2:3, :]
    y = (bg * conv).astype(BF16)
    o_ref[...] = x + jnp.dot(y, wout_ref[...], preferred_element_type=F32)


def _conv_mixer(h, gain, w_in, conv_k, w_out, seq_len):
    n, d = h.shape
    t = min(CONV_TOKENS, seq_len)
    assert seq_len % t == 0 and conv_k.shape[0] == CONV_WIDTH
    return pl.pallas_call(
        functools.partial(_conv_mixer_kernel, tiles_per_seq=seq_len // t),
        grid=(n // t,),
        in_specs=[
            pl.BlockSpec((t, d), lambda i: (i, 0)),
            pl.BlockSpec((1, d), lambda i: (0, 0)),
            pl.BlockSpec((d, 3 * d), lambda i: (0, 0)),
            pl.BlockSpec((CONV_WIDTH, d), lambda i: (0, 0)),
            pl.BlockSpec((d, d), lambda i: (0, 0)),
        ],
        out_specs=pl.BlockSpec((t, d), lambda i: (i, 0)),
        out_shape=jax.ShapeDtypeStruct((n, d), F32),
        scratch_shapes=[pltpu.VMEM((t + 2 * SUBLANES, d), F32)],
        compiler_params=_params("arbitrary"),
        name="conv_mixer",
    )(h, gain.reshape(1, d), w_in.astype(BF16), conv_k, w_out.astype(BF16))


def _top_planes(planes, ids, k):
    planes = list(planes)
    chunk = -(-len(planes) // TOPK_CHAINS)
    vals, picked = [], []
    for it in range(k):
        best = best_id = None
        for lo in range(0, len(planes), chunk):
            part = planes[lo]
            part_id = jnp.full(part.shape, ids[lo], F32)
            for plane, plane_id in zip(planes[lo + 1:lo + chunk], ids[lo + 1:lo + chunk]):
                better = plane > part
                part = jnp.where(better, plane, part)
                part_id = jnp.where(better, plane_id, part_id)
            if best is None:
                best, best_id = part, part_id
            else:
                better = part > best
                best = jnp.where(better, part, best)
                best_id = jnp.where(better, part_id, best_id)
        vals.append(best)
        picked.append(best_id)
        if it + 1 < k:
            planes = [jnp.where(best_id == plane_id, -jnp.inf, plane)
                      for plane, plane_id in zip(planes, ids)]
    return vals, picked


def _lookup(table, sel):
    out = jnp.zeros(sel.shape, F32)
    for a, plane in enumerate(table):
        out = out + jnp.where(sel == float(a), plane, 0.0)
    return out


def _score_planes(scores, scratch_ref):
    for g in range(SUBLANES):
        scratch_ref[g * SCORE_PITCH:g * SCORE_PITCH + PEER_N_KEYS, :] = (
            scores[:, g * LANES:(g + 1) * LANES])
    return [scratch_ref[pl.ds(n, SUBLANES, stride=SCORE_PITCH), :] for n in range(PEER_N_KEYS)]


def _peer_topk_kernel(h_ref, g_ref, wqt_ref, sk_ref, xn_ref, picks_ref, s0_ref, s1_ref):
    k = PEER_TOPK
    half = k // 2
    xn_ref[...] = (_rms_scale(h_ref[...]) * g_ref[...]).astype(BF16)
    key_ids = [float(n) for n in range(PEER_N_KEYS)]
    cand_ab = ([(0, b) for b in range(k)]
               + [(a, b) for a in range(1, half) for b in range(half)]
               + [(a, 0) for a in range(half, k)])
    cand_ids = [float(a * k + b) for a, b in cand_ab]

    def head_body(hd, carry):
        q0 = pl.multiple_of(hd * 2 * PEER_KEY_DIM, 2 * PEER_KEY_DIM)
        qt = lax.dot_general(wqt_ref[pl.ds(q0, 2 * PEER_KEY_DIM), :], xn_ref[...], _NT,
                             preferred_element_type=F32).astype(BF16)
        tops = []
        for p, scratch_ref in enumerate((s0_ref, s1_ref)):
            scores = jnp.dot(sk_ref[hd, p], qt[p * PEER_KEY_DIM:(p + 1) * PEER_KEY_DIM, :],
                             preferred_element_type=F32)
            tops.append(_top_planes(_score_planes(scores, scratch_ref), key_ids, k))
        (v1, i1), (v2, i2) = tops
        top_s, top_c = _top_planes([v1[a] + v2[b] for a, b in cand_ab], cand_ids, k)
        denom = None
        gates = []
        for j in range(k):
            gates.append(jnp.exp(top_s[j] - top_s[0]))
            denom = gates[j] if denom is None else denom + gates[j]
        for j in range(k):
            a_sel = jnp.floor(top_c[j] * (1.0 / k))
            b_sel = top_c[j] - a_sel * k
            row0 = pl.multiple_of((hd * k + j) * SUBLANES, SUBLANES)
            picks_ref[0, 0, pl.ds(row0, SUBLANES), :] = _lookup(i1, a_sel)
            picks_ref[0, 1, pl.ds(row0, SUBLANES), :] = _lookup(i2, b_sel)
            picks_ref[0, 2, pl.ds(row0, SUBLANES), :] = gates[j] / denom
        return carry

    lax.fori_loop(0, PEER_HEADS, head_body, 0)


def _peer_topk(h, gain, w_query, subkeys):
    n, d = h.shape
    t = TOPK_TOKENS
    qdim = PEER_HEADS * 2 * PEER_KEY_DIM
    n_picks = PEER_HEADS * PEER_TOPK
    assert subkeys.shape == (PEER_HEADS, 2, PEER_N_KEYS, PEER_KEY_DIM)
    assert w_query.shape == (d, qdim) and n % t == 0
    return pl.pallas_call(
        _peer_topk_kernel,
        grid=(n // t,),
        in_specs=[
            pl.BlockSpec((t, d), lambda i: (i, 0)),
            pl.BlockSpec((1, d), lambda i: (0, 0)),
            pl.BlockSpec((qdim, d), lambda i: (0, 0)),
            pl.BlockSpec((PEER_HEADS, 2, PEER_N_KEYS, PEER_KEY_DIM), lambda i: (0, 0, 0, 0)),
        ],
        out_specs=[
            pl.BlockSpec((t, d), lambda i: (i, 0)),
            pl.BlockSpec((1, 3, n_picks * SUBLANES, LANES), lambda i: (i, 0, 0, 0)),
        ],
        out_shape=[
            jax.ShapeDtypeStruct((n, d), BF16),
            jax.ShapeDtypeStruct((n // t, 3, n_picks * SUBLANES, LANES), F32),
        ],
        scratch_shapes=[pltpu.VMEM((SUBLANES * SCORE_PITCH, LANES), F32)] * 2,
        compiler_params=_params("arbitrary"),
        name="peer_topk",
    )(h, gain.reshape(1, d), w_query.T.astype(BF16), subkeys.astype(BF16))


def _pack_bf16_pair(lo, hi):
    lo_bits = pltpu.bitcast(lo.astype(BF16).astype(F32), jnp.uint32)
    hi_bits = pltpu.bitcast(hi.astype(BF16).astype(F32), jnp.uint32)
    return lax.shift_right_logical(lo_bits, jnp.uint32(16)) | (hi_bits & jnp.uint32(0xFFFF0000))


def _unpack_bf16_pair(words, which):
    if which == 0:
        return pltpu.bitcast(lax.shift_left(words, jnp.uint32(16)), F32)
    return pltpu.bitcast(words & jnp.uint32(0xFFFF0000), F32)


def _gate_build_kernel(picks_ref, gate_ref, i1t_s, i2t_s, gtt_s):
    n_picks = picks_ref.shape[2] // SUBLANES
    group = pl.program_id(1)
    for which, dst in enumerate((i1t_s, i2t_s, gtt_s)):
        dst[...] = picks_ref[0, which, pl.ds(group, n_picks, stride=SUBLANES), :].T
    sub_id = lax.broadcasted_iota(jnp.int32, (PEER_N_KEYS, n_picks), 0).astype(F32)

    def token_body(c, carry):
        i1r = i1t_s[pl.ds(c, 1), :]
        i2r = i2t_s[pl.ds(c, 1), :]
        gr = gtt_s[pl.ds(c, 1), :]
        a = jnp.where(sub_id == i1r, gr, 0.0).astype(BF16)
        b = jnp.where(sub_id == i2r, 1.0, 0.0).astype(BF16)
        g = lax.dot_general(a, b, _NT, preferred_element_type=F32)
        g = g.reshape(N_EXPERT_BLOCKS, EXPERT_ROWS, PEER_N_KEYS)
        gate_ref[:, c, :, :] = _pack_bf16_pair(g[:, 0:SUBLANES, :], g[:, SUBLANES:, :])
        return carry

    lax.fori_loop(0, GATE_TOKENS, token_body, 0, unroll=GATE_UNROLL)


def _gate_build(picks):
    tiles, _, pick_rows, _ = picks.shape
    n = tiles * TOPK_TOKENS
    n_picks = pick_rows // SUBLANES
    return pl.pallas_call(
        _gate_build_kernel,
        grid=(tiles, SUBLANES),
        in_specs=[pl.BlockSpec((1, 3, pick_rows, LANES), lambda i, g: (i, 0, 0, 0))],
        out_specs=pl.BlockSpec((N_EXPERT_BLOCKS, GATE_TOKENS, SUBLANES, PEER_N_KEYS),
                               lambda i, g: (0, i * SUBLANES + g, 0, 0)),
        out_shape=jax.ShapeDtypeStruct((N_EXPERT_BLOCKS, n, SUBLANES, PEER_N_KEYS), jnp.uint32),
        scratch_shapes=[pltpu.VMEM((GATE_TOKENS, n_picks), F32)] * 3,
        compiler_params=_params("arbitrary", "arbitrary"),
        name="peer_gate_build",
    )(picks)


def _peer_expert_kernel(xn_ref, gate_ref, u_ref, v_ref, h_ref, fg_ref, o_ref, coef_ref,
                        *, final_norm):
    e = pl.program_id(1)
    t = xn_ref.shape[0]

    @pl.when(e == 0)
    def _():
        o_ref[...] = h_ref[...]

    xn = xn_ref[...]
    per_chunk = EXPERT_CHUNK // PEER_N_KEYS
    for q in range(EXPERT_BLOCK // EXPERT_CHUNK):
        cols = slice(q * EXPERT_CHUNK, (q + 1) * EXPERT_CHUNK)
        act = lax.dot_general(xn, u_ref[cols, :], _NT, preferred_element_type=F32)
        gelu = 0.5 * act * (1.0 + lax.erf(act * (1.0 / math.sqrt(2.0))))
        gate = jnp.concatenate(
            [_unpack_bf16_pair(gate_ref[0, pl.ds(r % SUBLANES, t, stride=SUBLANES), :],
                               r // SUBLANES)
             for r in range(q * per_chunk, (q + 1) * per_chunk)], axis=1)
        coef_ref[:, cols] = (gate * gelu).astype(BF16)
    o_ref[...] += jnp.dot(coef_ref[...], v_ref[...], preferred_element_type=F32)

    if final_norm:
        @pl.when(e == pl.num_programs(1) - 1)
        def _():
            o_ref[...] = _rms_scale(o_ref[...]) * fg_ref[...]


def _peer_experts(xn, gate, u_tbl, v_tbl, h, final_gain, final_norm):
    n, d = h.shape
    t = min(EXPERT_TOKENS, n)
    assert n % t == 0 and u_tbl.shape == (N_EXPERT_BLOCKS * EXPERT_BLOCK, d)
    gate2 = gate.reshape(gate.shape[0], n * SUBLANES, PEER_N_KEYS)
    return pl.pallas_call(
        functools.partial(_peer_expert_kernel, final_norm=final_norm),
        grid=(n // t, N_EXPERT_BLOCKS),
        in_specs=[
            pl.BlockSpec((t, d), lambda i, e: (i, 0)),
            pl.BlockSpec((1, t * SUBLANES, PEER_N_KEYS), lambda i, e: (e, i, 0)),
            pl.BlockSpec((EXPERT_BLOCK, d), lambda i, e: (e, 0)),
            pl.BlockSpec((EXPERT_BLOCK, d), lambda i, e: (e, 0)),
            pl.BlockSpec((t, d), lambda i, e: (i, 0)),
            pl.BlockSpec((1, d), lambda i, e: (0, 0)),
        ],
        out_specs=pl.BlockSpec((t, d), lambda i, e: (i, 0)),
        out_shape=jax.ShapeDtypeStruct((n, d), F32),
        scratch_shapes=[pltpu.VMEM((t, EXPERT_BLOCK), BF16)],
        compiler_params=_params("arbitrary", "arbitrary"),
        name="peer_experts",
    )(xn, gate2, u_tbl.astype(BF16), v_tbl.astype(BF16), h, final_gain.reshape(1, d))


def _peer_ffn(h, gain, w_query, subkeys, u_tbl, v_tbl, final_gain, final_norm):
    xn, picks = _peer_topk(h, gain, w_query, subkeys)
    return _peer_experts(xn, _gate_build(picks), u_tbl, v_tbl, h, final_gain, final_norm)


def _kvq_kernel(h_ref, hp_ref, gkv_ref, gq_ref, wk_ref, wvt_ref, wqt_ref,
                k_ref, vt_ref, qt_ref):
    n_heads = qt_ref.shape[1]
    t = h_ref.shape[0]
    xkv = (_rms_scale(hp_ref[...]) * gkv_ref[...]).astype(BF16)
    xq = (_rms_scale(h_ref[...]) * gq_ref[...]).astype(BF16)
    k_ref[...] = jnp.dot(xkv, wk_ref[...], preferred_element_type=F32).astype(BF16)
    vt_ref[0] = lax.dot_general(wvt_ref[...], xkv, _NT,
                                preferred_element_type=F32).astype(BF16)
    qt = (lax.dot_general(wqt_ref[...], xq, _NT, preferred_element_type=F32)
          * (0.5 * HEAD_DIM ** -0.5)).astype(BF16)
    zeros = jnp.zeros((HEAD_DIM, t), BF16)
    for hd in range(n_heads):
        q_h = qt[hd * HEAD_DIM:(hd + 1) * HEAD_DIM, :]
        if hd % 2 == 0:
            qt_ref[0, hd, 0:HEAD_DIM, :] = q_h
            qt_ref[0, hd, HEAD_DIM:2 * HEAD_DIM, :] = zeros
        else:
            qt_ref[0, hd, 0:HEAD_DIM, :] = zeros
            qt_ref[0, hd, HEAD_DIM:2 * HEAD_DIM, :] = q_h


def _key_permute(x, seq_len):
    n, d = x.shape
    kb = min(ATTN_BLOCK, seq_len)
    seg = kb // SUBLANES
    return x.reshape(n // kb, SUBLANES, seg, d).swapaxes(1, 2).reshape(n, d)


def _kvq_proj(h, g_kv, g_q, w_kv, w_q, batch, seq_len):
    n, d = h.shape
    n_heads = d // HEAD_DIM
    t = min(PROJ_TOKENS, seq_len)
    tiles = seq_len // t
    assert seq_len % t == 0 and t % min(ATTN_BLOCK, seq_len) == 0 and n_heads % 2 == 0
    w_k, w_v = w_kv[:, :d], w_kv[:, d:]
    return pl.pallas_call(
        _kvq_kernel,
        grid=(n // t,),
        in_specs=[
            pl.BlockSpec((t, d), lambda i: (i, 0)),
            pl.BlockSpec((t, d), lambda i: (i, 0)),
            pl.BlockSpec((1, d), lambda i: (0, 0)),
            pl.BlockSpec((1, d), lambda i: (0, 0)),
            pl.BlockSpec((d, d), lambda i: (0, 0)),
            pl.BlockSpec((d, d), lambda i: (0, 0)),
            pl.BlockSpec((d, d), lambda i: (0, 0)),
        ],
        out_specs=[
            pl.BlockSpec((t, d), lambda i: (i, 0)),
            pl.BlockSpec((1, d, t), lambda i: (i // tiles, 0, i % tiles)),
            pl.BlockSpec((1, n_heads, 2 * HEAD_DIM, t), lambda i: (i // tiles, 0, 0, i % tiles)),
        ],
        out_shape=[
            jax.ShapeDtypeStruct((n, d), BF16),
            jax.ShapeDtypeStruct((batch, d, seq_len), BF16),
            jax.ShapeDtypeStruct((batch, n_heads, 2 * HEAD_DIM, seq_len), BF16),
        ],
        compiler_params=_params("arbitrary"),
        name="kvq_proj",
    )(h, _key_permute(h, seq_len), g_kv.reshape(1, d), g_q.reshape(1, d),
      w_k.astype(BF16), w_v.T.astype(BF16), w_q.T.astype(BF16))


def _attn_kernel(qt_ref, k_ref, vt_ref, bias_ref, o_ref, *, blk, tiles):
    seg = blk // SUBLANES
    width = 2 * blk
    first_tile = pl.program_id(2) * tiles
    qts = [jnp.concatenate([qt_ref[0, 0, :, u * blk:(u + 1) * blk],
                            qt_ref[0, 1, :, u * blk:(u + 1) * blk]], axis=1)
           for u in range(tiles)]

    def block(u, kb, carry, acc0, acc1, masked):
        k0 = pl.multiple_of(kb * blk, blk)
        zh = jnp.dot(k_ref[0, pl.ds(k0, blk), :], qts[u], preferred_element_type=F32)
        if masked:
            zh = zh + bias_ref[...]
        ht = 0.5 * jnp.tanh(zh)
        beta = 0.5 + ht
        keep = 0.5 - ht
        run = jnp.ones((SUBLANES, width), F32)
        parts = [None] * seg
        for rr in reversed(range(seg)):
            parts[rr] = run
            run = run * keep[rr * SUBLANES:(rr + 1) * SUBLANES, :]
        offs = [None] * SUBLANES
        c = carry
        for s in reversed(range(SUBLANES)):
            offs[s] = c
            c = c * run[s:s + 1, :]
        off = jnp.concatenate(offs, axis=0)
        w = jnp.concatenate(
            [beta[rr * SUBLANES:(rr + 1) * SUBLANES, :] * (parts[rr] * off)
             for rr in range(seg)], axis=0).astype(BF16)
        vt = vt_ref[0, :, pl.ds(k0, blk)]
        acc0 = acc0 + jnp.dot(vt[0:HEAD_DIM, :], w[:, 0:blk], preferred_element_type=F32)
        acc1 = acc1 + jnp.dot(vt[HEAD_DIM:, :], w[:, blk:], preferred_element_type=F32)
        return c, acc0, acc1

    ones = jnp.ones((1, width), F32)
    zero = jnp.zeros((HEAD_DIM, blk), F32)
    state = (jnp.int32(0),)
    for u in range(tiles):
        state = state + block(u, first_tile + u, ones, zero, zero, True)

    def live_carries(st):
        return [jnp.where(st[0] < first_tile + u, st[1 + 3 * u], 0.0) for u in range(tiles)]

    def cond(st):
        return jnp.max(functools.reduce(jnp.maximum, live_carries(st))) > 0.0

    def body(st):
        new = (st[0] + 1,)
        for u, carry in enumerate(live_carries(st)):
            kb = jnp.maximum(first_tile + u - 1 - st[0], 0)
            new = new + block(u, kb, carry, st[2 + 3 * u], st[3 + 3 * u], False)
        return new

    state = lax.while_loop(cond, body, state)
    for u in range(tiles):
        o_ref[0, 0:HEAD_DIM, u * blk:(u + 1) * blk] = state[2 + 3 * u]
        o_ref[0, HEAD_DIM:, u * blk:(u + 1) * blk] = state[3 + 3 * u]


def _causal_bias(blk):
    seg = blk // SUBLANES
    row = jnp.arange(blk, dtype=jnp.int32)[:, None]
    key_local = (row % SUBLANES) * seg + row // SUBLANES
    query_local = jnp.arange(2 * blk, dtype=jnp.int32)[None, :] % blk
    return jnp.where(key_local < query_local, 0.0, -1e30).astype(F32)


def _stick_breaking_attention(qt_pad, k_perm, vt_perm, seq_len):
    batch, n_heads = qt_pad.shape[0], qt_pad.shape[1]
    d = n_heads * HEAD_DIM
    blk = min(ATTN_BLOCK, seq_len)
    tiles = min(ATTN_TILES, seq_len // blk)
    assert seq_len % (blk * tiles) == 0
    k3 = k_perm.reshape(batch, seq_len, d)
    return pl.pallas_call(
        functools.partial(_attn_kernel, blk=blk, tiles=tiles),
        grid=(batch, n_heads // 2, seq_len // (blk * tiles)),
        in_specs=[
            pl.BlockSpec((1, 2, 2 * HEAD_DIM, blk * tiles), lambda b, p, q: (b, p, 0, q)),
            pl.BlockSpec((1, seq_len, 2 * HEAD_DIM), lambda b, p, q: (b, 0, p)),
            pl.BlockSpec((1, 2 * HEAD_DIM, seq_len), lambda b, p, q: (b, p, 0)),
            pl.BlockSpec((blk, 2 * blk), lambda b, p, q: (0, 0)),
        ],
        out_specs=pl.BlockSpec((1, 2 * HEAD_DIM, blk * tiles), lambda b, p, q: (b, p, q)),
        out_shape=jax.ShapeDtypeStruct((batch, d, seq_len), F32),
        compiler_params=_params("arbitrary", "arbitrary", "arbitrary"),
        name="stick_breaking_attention",
    )(qt_pad, k3, vt_perm, _causal_bias(blk))


def _out_proj_kernel(ot_ref, w_ref, h_ref, o_ref):
    a = ot_ref[0].T.astype(BF16)
    o_ref[...] = h_ref[...] + jnp.dot(a, w_ref[...], preferred_element_type=F32)


def _out_proj(ot, w_o, h, seq_len):
    n, d = h.shape
    t = min(PROJ_TOKENS, seq_len)
    tiles = seq_len // t
    return pl.pallas_call(
        _out_proj_kernel,
        grid=(n // t,),
        in_specs=[
            pl.BlockSpec((1, d, t), lambda i: (i // tiles, 0, i % tiles)),
            pl.BlockSpec((d, d), lambda i: (0, 0)),
            pl.BlockSpec((t, d), lambda i: (i, 0)),
        ],
        out_specs=pl.BlockSpec((t, d), lambda i: (i, 0)),
        out_shape=jax.ShapeDtypeStruct((n, d), F32),
        compiler_params=_params("arbitrary"),
        name="attn_out_proj",
    )(ot, w_o.astype(BF16), h)


def kernel(x, norm_mix, norm_ffn, conv_w_in, conv_kernel, conv_w_out, norm_kv, w_kv,
           attn_w_q, attn_w_o, peer_w_query, peer_subkeys, peer_u, peer_v, norm_final):
    batch, seq_len, d = x.shape
    assert norm_mix.shape[0] == 2 and conv_w_in.shape[0] == 1 and attn_w_q.shape[0] == 1
    h = x.reshape(batch * seq_len, d)
    h = _conv_mixer(h, norm_mix[0], conv_w_in[0], conv_kernel[0], conv_w_out[0], seq_len)
    h = _peer_ffn(h, norm_ffn[0], peer_w_query[0], peer_subkeys[0], peer_u[0], peer_v[0],
                  norm_final, final_norm=False)
    k_perm, vt_perm, qt_pad = _kvq_proj(h, norm_kv, norm_mix[1], w_kv, attn_w_q[0],
                                        batch, seq_len)
    ot = _stick_breaking_attention(qt_pad, k_perm, vt_perm, seq_len)
    h = _out_proj(ot, attn_w_o[0], h, seq_len)
    h = _peer_ffn(h, norm_ffn[1], peer_w_query[1], peer_subkeys[1], peer_u[1], peer_v[1],
                  norm_final, final_norm=True)
    return h.reshape(batch, seq_len, d)
```

```python
import functools
import math

import jax
import jax.numpy as jnp
from jax import lax
from jax.experimental import pallas as pl
from jax.experimental.pallas import tpu as pltpu

F32 = jnp.float32
BF16 = jnp.bfloat16

RMS_EPS = 1e-6
CONV_WIDTH = 3
HEAD_DIM = 64
PEER_HEADS = 8
PEER_N_KEYS = 128
PEER_KEY_DIM = 128
PEER_TOPK = 16

SUBLANES = 8
LANES = 128
VMEM_LIMIT_BYTES = 56 * 1024 * 1024

CONV_TOKENS = 512
TOPK_TOKENS = SUBLANES * LANES
SCORE_PITCH = PEER_N_KEYS + SUBLANES
TOPK_CHAINS = 4
GATE_TOKENS = LANES
GATE_UNROLL = 32
EXPERT_TOKENS = 1024
EXPERT_ROWS = 2 * SUBLANES
EXPERT_BLOCK = EXPERT_ROWS * PEER_N_KEYS
N_EXPERT_BLOCKS = PEER_N_KEYS // EXPERT_ROWS
EXPERT_CHUNK = 2 * PEER_N_KEYS
PROJ_TOKENS = 512
ATTN_BLOCK = 256
ATTN_TILES = 2

_NT = (((1,), (1,)), ((), ()))


def _params(*sem):
    return pltpu.CompilerParams(dimension_semantics=sem,
                                vmem_limit_bytes=VMEM_LIMIT_BYTES)


def _rms_scale(x):
    return x * lax.rsqrt(jnp.mean(x * x, axis=-1, keepdims=True) + RMS_EPS)


def _conv_mixer_kernel(x_ref, g_ref, win_ref, ck_ref, wout_ref, o_ref, ubuf_ref,
                       *, tiles_per_seq):
    t = x_ref.shape[0]
    d = x_ref.shape[1]
    halo = SUBLANES
    first = (pl.program_id(0) % tiles_per_seq) == 0

    @pl.when(first)
    def _():
        ubuf_ref[0:halo, :] = jnp.zeros((halo, d), F32)

    @pl.when(jnp.logical_not(first))
    def _():
        ubuf_ref[0:halo, :] = ubuf_ref[t:t + halo, :]

    x = x_ref[...]
    xb = (_rms_scale(x) * g_ref[...]).astype(BF16)
    bg = jnp.dot(xb, win_ref[:, 0:d], preferred_element_type=F32)
    cg = jnp.dot(xb, win_ref[:, d:2 * d], preferred_element_type=F32)
    hv = jnp.dot(xb, win_ref[:, 2 * d:3 * d], preferred_element_type=F32)
    u = cg * hv
    ubuf_ref[halo:halo + t, :] = u
    conv = ubuf_ref[halo - 2:halo - 2 + t, :] * ck_ref[0:1, :]
    conv = conv + ubuf_ref[halo - 1:halo - 1 + t, :] * ck_ref[1:2, :]
    conv = conv + u * ck_ref[2:3, :]
    y = (bg * conv).astype(BF16)
    o_ref[...] = x + jnp.dot(y, wout_ref[...], preferred_element_type=F32)


def _conv_mixer(h, gain, w_in, conv_k, w_out, seq_len):
    n, d = h.shape
    t = min(CONV_TOKENS, seq_len)
    assert seq_len % t == 0 and conv_k.shape[0] == CONV_WIDTH
    return pl.pallas_call(
        functools.partial(_conv_mixer_kernel, tiles_per_seq=seq_len // t),
        grid=(n // t,),
        in_specs=[
            pl.BlockSpec((t, d), lambda i: (i, 0)),
            pl.BlockSpec((1, d), lambda i: (0, 0)),
            pl.BlockSpec((d, 3 * d), lambda i: (0, 0)),
            pl.BlockSpec((CONV_WIDTH, d), lambda i: (0, 0)),
            pl.BlockSpec((d, d), lambda i: (0, 0)),
        ],
        out_specs=pl.BlockSpec((t, d), lambda i: (i, 0)),
        out_shape=jax.ShapeDtypeStruct((n, d), F32),
        scratch_shapes=[pltpu.VMEM((t + 2 * SUBLANES, d), F32)],
        compiler_params=_params("arbitrary"),
        name="conv_mixer",
    )(h, gain.reshape(1, d), w_in.astype(BF16), conv_k, w_out.astype(BF16))


def _top_planes(planes, ids, k):
    planes = list(planes)
    chunk = -(-len(planes) // TOPK_CHAINS)
    vals, picked = [], []
    for it in range(k):
        best = best_id = None
        for lo in range(0, len(planes), chunk):
            part = planes[lo]
            part_id = jnp.full(part.shape, ids[lo], F32)
            for plane, plane_id in zip(planes[lo + 1:lo + chunk], ids[lo + 1:lo + chunk]):
                better = plane > part
                part = jnp.where(better, plane, part)
                part_id = jnp.where(better, plane_id, part_id)
            if best is None:
                best, best_id = part, part_id
            else:
                better = part > best
                best = jnp.where(better, part, best)
                best_id = jnp.where(better, part_id, best_id)
        vals.append(best)
        picked.append(best_id)
        if it + 1 < k:
            planes = [jnp.where(best_id == plane_id, -jnp.inf, plane)
                      for plane, plane_id in zip(planes, ids)]
    return vals, picked


def _sort_network(n):
    pairs = []

    def merge(lo, count, stride):
        step = stride * 2
        if step < count:
            merge(lo, count, step)
            merge(lo + stride, count, step)
            pairs.extend((i, i + stride) for i in range(lo + stride, lo + count - stride, step))
        else:
            pairs.append((lo, lo + stride))

    def sort(lo, count):
        if count > 1:
            sort(lo, count // 2)
            sort(lo + count // 2, count // 2)
            merge(lo, count, 1)

    sort(0, n)
    return pairs


def _order(vals, ids, i, j):
    swap = vals[j] > vals[i]
    vals[i], vals[j] = jnp.where(swap, vals[j], vals[i]), jnp.where(swap, vals[i], vals[j])
    ids[i], ids[j] = jnp.where(swap, ids[j], ids[i]), jnp.where(swap, ids[i], ids[j])


def _top_planes_sorted(planes, ids, k):
    assert len(planes) % k == 0 and k & (k - 1) == 0
    network = _sort_network(k)
    groups = []
    for lo in range(0, len(planes), k):
        vals = list(planes[lo:lo + k])
        gids = [jnp.full(vals[0].shape, i, F32) for i in ids[lo:lo + k]]
        for i, j in network:
            _order(vals, gids, i, j)
        groups.append((vals, gids))
    return _merge_runs(groups, planes, k)


def _bitonic_merge(vals, ids):
    dist = len(vals) // 2
    while dist:
        for i in range(len(vals)):
            if i & dist == 0:
                _order(vals, ids, i, i + dist)
        dist //= 2


def _merge_runs(runs, planes, k):
    while len(runs) > 1:
        merged = [runs[-1]] if len(runs) % 2 else []
        for (va, ia), (vb, ib) in zip(runs[0:-1:2], runs[1::2]):
            take_b = [vb[k - 1 - i] > va[i] for i in range(k)]
            vals = [jnp.where(t, vb[k - 1 - i], va[i]) for i, t in enumerate(take_b)]
            ids = [jnp.where(t, ib[k - 1 - i], ia[i]) for i, t in enumerate(take_b)]
            _bitonic_merge(vals, ids)
            merged.append((vals, ids))
        runs = merged
    vals, ids = runs[0]
    tied = jnp.zeros(vals[0].shape, F32)
    for i in range(k - 1):
        tied = jnp.where(vals[i] > vals[i + 1], tied, 1.0)
    reach = jnp.zeros(vals[0].shape, F32)
    for plane in planes:
        reach = reach + jnp.where(plane >= vals[k - 1], 1.0, 0.0)
    tied = jnp.where(reach == float(k), tied, 1.0)
    return vals, ids, tied


def _top_sums_sorted(cands, cand_ids, k):
    half = k // 2
    as_run = lambda lo, hi: (list(cands[lo:hi]), [jnp.full(cands[0].shape, i, F32)
                                                  for i in cand_ids[lo:hi]])
    runs = [as_run(0, k)]
    shorts = [as_run(lo, lo + half) for lo in range(k, len(cands), half)]
    assert len(shorts) % 2 == 0
    for (va, ia), (vb, ib) in zip(shorts[0::2], shorts[1::2]):
        vals, ids = va + vb[::-1], ia + ib[::-1]
        _bitonic_merge(vals, ids)
        runs.append((vals, ids))
    return _merge_runs(runs, cands, k)


def _lookup(table, sel):
    out = jnp.zeros(sel.shape, F32)
    for a, plane in enumerate(table):
        out = out + jnp.where(sel == float(a), plane, 0.0)
    return out


def _score_planes(scores, scratch_ref):
    for g in range(SUBLANES):
        scratch_ref[g * SCORE_PITCH:g * SCORE_PITCH + PEER_N_KEYS, :] = (
            scores[:, g * LANES:(g + 1) * LANES])
    return [scratch_ref[pl.ds(n, SUBLANES, stride=SCORE_PITCH), :] for n in range(PEER_N_KEYS)]


def _peer_topk_kernel(h_ref, g_ref, wqt_ref, sk_ref, xn_ref, picks_ref, s0_ref, s1_ref):
    k = PEER_TOPK
    half = k // 2
    xn_ref[...] = (_rms_scale(h_ref[...]) * g_ref[...]).astype(BF16)
    key_ids = [float(n) for n in range(PEER_N_KEYS)]
    cand_ab = ([(0, b) for b in range(k)]
               + [(a, b) for a in range(1, half) for b in range(half)]
               + [(a, 0) for a in range(half, k)])
    cand_ids = [float(a * k + b) for a, b in cand_ab]

    def head_body(hd, carry):
        q0 = pl.multiple_of(hd * 2 * PEER_KEY_DIM, 2 * PEER_KEY_DIM)
        qt = lax.dot_general(wqt_ref[pl.ds(q0, 2 * PEER_KEY_DIM), :], xn_ref[...], _NT,
                             preferred_element_type=F32).astype(BF16)
        planes = []
        for p, scratch_ref in enumerate((s0_ref, s1_ref)):
            scores = jnp.dot(sk_ref[hd, p], qt[p * PEER_KEY_DIM:(p + 1) * PEER_KEY_DIM, :],
                             preferred_element_type=F32)
            planes.append(_score_planes(scores, scratch_ref))
        v1, i1, tied1 = _top_planes_sorted(planes[0], key_ids, k)
        v2, i2, tied2 = _top_planes_sorted(planes[1], key_ids, k)

        def exact_scan():
            return _top_planes(planes[0], key_ids, k) + _top_planes(planes[1], key_ids, k)

        v1, i1, v2, i2 = lax.cond(jnp.max(jnp.maximum(tied1, tied2)) > 0.0,
                                  exact_scan, lambda: (v1, i1, v2, i2))
        cands = [v1[a] + v2[b] for a, b in cand_ab]
        top_s, top_c, tied = _top_sums_sorted(cands, cand_ids, k)
        top_s, top_c = lax.cond(jnp.max(tied) > 0.0,
                                lambda: _top_planes(cands, cand_ids, k),
                                lambda: (top_s, top_c))
        denom = None
        gates = []
        for j in range(k):
            gates.append(jnp.exp(top_s[j] - top_s[0]))
            denom = gates[j] if denom is None else denom + gates[j]
        for j in range(k):
            a_sel = jnp.floor(top_c[j] * (1.0 / k))
            b_sel = top_c[j] - a_sel * k
            row0 = pl.multiple_of((hd * k + j) * SUBLANES, SUBLANES)
            picks_ref[0, 0, pl.ds(row0, SUBLANES), :] = _lookup(i1, a_sel)
            picks_ref[0, 1, pl.ds(row0, SUBLANES), :] = _lookup(i2, b_sel)
            picks_ref[0, 2, pl.ds(row0, SUBLANES), :] = gates[j] / denom
        return carry

    lax.fori_loop(0, PEER_HEADS, head_body, 0)


def _peer_topk(h, gain, w_query, subkeys):
    n, d = h.shape
    t = TOPK_TOKENS
    qdim = PEER_HEADS * 2 * PEER_KEY_DIM
    n_picks = PEER_HEADS * PEER_TOPK
    assert subkeys.shape == (PEER_HEADS, 2, PEER_N_KEYS, PEER_KEY_DIM)
    assert w_query.shape == (d, qdim) and n % t == 0
    return pl.pallas_call(
        _peer_topk_kernel,
        grid=(n // t,),
        in_specs=[
            pl.BlockSpec((t, d), lambda i: (i, 0)),
            pl.BlockSpec((1, d), lambda i: (0, 0)),
            pl.BlockSpec((qdim, d), lambda i: (0, 0)),
            pl.BlockSpec((PEER_HEADS, 2, PEER_N_KEYS, PEER_KEY_DIM), lambda i: (0, 0, 0, 0)),
        ],
        out_specs=[
            pl.BlockSpec((t, d), lambda i: (i, 0)),
            pl.BlockSpec((1, 3, n_picks * SUBLANES, LANES), lambda i: (i, 0, 0, 0)),
        ],
        out_shape=[
            jax.ShapeDtypeStruct((n, d), BF16),
            jax.ShapeDtypeStruct((n // t, 3, n_picks * SUBLANES, LANES), F32),
        ],
        scratch_shapes=[pltpu.VMEM((SUBLANES * SCORE_PITCH, LANES), F32)] * 2,
        compiler_params=_params("arbitrary"),
        name="peer_topk",
    )(h, gain.reshape(1, d), w_query.T.astype(BF16), subkeys.astype(BF16))


def _pack_bf16_pair(lo, hi):
    lo_bits = pltpu.bitcast(lo.astype(BF16).astype(F32), jnp.uint32)
    hi_bits = pltpu.bitcast(hi.astype(BF16).astype(F32), jnp.uint32)
    return lax.shift_right_logical(lo_bits, jnp.uint32(16)) | (hi_bits & jnp.uint32(0xFFFF0000))


def _unpack_bf16_pair(words, which):
    if which == 0:
        return pltpu.bitcast(lax.shift_left(words, jnp.uint32(16)), F32)
    return pltpu.bitcast(words & jnp.uint32(0xFFFF0000), F32)


def _gate_build_kernel(picks_ref, gate_ref, i1t_s, i2t_s, gtt_s):
    n_picks = picks_ref.shape[2] // SUBLANES
    group = pl.program_id(1)
    for which, dst in enumerate((i1t_s, i2t_s, gtt_s)):
        dst[...] = picks_ref[0, which, pl.ds(group, n_picks, stride=SUBLANES), :].T
    sub_id = lax.broadcasted_iota(jnp.int32, (PEER_N_KEYS, n_picks), 0).astype(F32)

    def token_body(c, carry):
        i1r = i1t_s[pl.ds(c, 1), :]
        i2r = i2t_s[pl.ds(c, 1), :]
        gr = gtt_s[pl.ds(c, 1), :]
        a = jnp.where(sub_id == i1r, gr, 0.0).astype(BF16)
        b = jnp.where(sub_id == i2r, 1.0, 0.0).astype(BF16)
        g = lax.dot_general(a, b, _NT, preferred_element_type=F32)
        g = g.reshape(N_EXPERT_BLOCKS, EXPERT_ROWS, PEER_N_KEYS)
        gate_ref[:, c, :, :] = _pack_bf16_pair(g[:, 0:SUBLANES, :], g[:, SUBLANES:, :])
        return carry

    lax.fori_loop(0, GATE_TOKENS, token_body, 0, unroll=GATE_UNROLL)


def _gate_build(picks):
    tiles, _, pick_rows, _ = picks.shape
    n = tiles * TOPK_TOKENS
    n_picks = pick_rows // SUBLANES
    return pl.pallas_call(
        _gate_build_kernel,
        grid=(tiles, SUBLANES),
        in_specs=[pl.BlockSpec((1, 3, pick_rows, LANES), lambda i, g: (i, 0, 0, 0))],
        out_specs=pl.BlockSpec((N_EXPERT_BLOCKS, GATE_TOKENS, SUBLANES, PEER_N_KEYS),
                               lambda i, g: (0, i * SUBLANES + g, 0, 0)),
        out_shape=jax.ShapeDtypeStruct((N_EXPERT_BLOCKS, n, SUBLANES, PEER_N_KEYS), jnp.uint32),
        scratch_shapes=[pltpu.VMEM((GATE_TOKENS, n_picks), F32)] * 3,
        compiler_params=_params("arbitrary", "arbitrary"),
        name="peer_gate_build",
    )(picks)


def _peer_expert_kernel(xn_ref, gate_ref, u_ref, v_ref, h_ref, fg_ref, o_ref, coef_ref,
                        *, final_norm):
    e = pl.program_id(1)
    t = xn_ref.shape[0]

    @pl.when(e == 0)
    def _():
        o_ref[...] = h_ref[...]

    xn = xn_ref[...]
    per_chunk = EXPERT_CHUNK // PEER_N_KEYS
    for q in range(EXPERT_BLOCK // EXPERT_CHUNK):
        cols = slice(q * EXPERT_CHUNK, (q + 1) * EXPERT_CHUNK)
        act = lax.dot_general(xn, u_ref[cols, :], _NT, preferred_element_type=F32)
        gelu = 0.5 * act * (1.0 + lax.erf(act * (1.0 / math.sqrt(2.0))))
        gate = jnp.concatenate(
            [_unpack_bf16_pair(gate_ref[0, pl.ds(r % SUBLANES, t, stride=SUBLANES), :],
                               r // SUBLANES)
             for r in range(q * per_chunk, (q + 1) * per_chunk)], axis=1)
        coef_ref[:, cols] = (gate * gelu).astype(BF16)
    o_ref[...] += jnp.dot(coef_ref[...], v_ref[...], preferred_element_type=F32)

    if final_norm:
        @pl.when(e == pl.num_programs(1) - 1)
        def _():
            o_ref[...] = _rms_scale(o_ref[...]) * fg_ref[...]


def _peer_experts(xn, gate, u_tbl, v_tbl, h, final_gain, final_norm):
    n, d = h.shape
    t = min(EXPERT_TOKENS, n)
    assert n % t == 0 and u_tbl.shape == (N_EXPERT_BLOCKS * EXPERT_BLOCK, d)
    gate2 = gate.reshape(gate.shape[0], n * SUBLANES, PEER_N_KEYS)
    return pl.pallas_call(
        functools.partial(_peer_expert_kernel, final_norm=final_norm),
        grid=(n // t, N_EXPERT_BLOCKS),
        in_specs=[
            pl.BlockSpec((t, d), lambda i, e: (i, 0)),
            pl.BlockSpec((1, t * SUBLANES, PEER_N_KEYS), lambda i, e: (e, i, 0)),
            pl.BlockSpec((EXPERT_BLOCK, d), lambda i, e: (e, 0)),
            pl.BlockSpec((EXPERT_BLOCK, d), lambda i, e: (e, 0)),
            pl.BlockSpec((t, d), lambda i, e: (i, 0)),
            pl.BlockSpec((1, d), lambda i, e: (0, 0)),
        ],
        out_specs=pl.BlockSpec((t, d), lambda i, e: (i, 0)),
        out_shape=jax.ShapeDtypeStruct((n, d), F32),
        scratch_shapes=[pltpu.VMEM((t, EXPERT_BLOCK), BF16)],
        compiler_params=_params("arbitrary", "arbitrary"),
        name="peer_experts",
    )(xn, gate2, u_tbl.astype(BF16), v_tbl.astype(BF16), h, final_gain.reshape(1, d))


def _peer_ffn(h, gain, w_query, subkeys, u_tbl, v_tbl, final_gain, final_norm):
    xn, picks = _peer_topk(h, gain, w_query, subkeys)
    return _peer_experts(xn, _gate_build(picks), u_tbl, v_tbl, h, final_gain, final_norm)


def _kvq_kernel(h_ref, hp_ref, gkv_ref, gq_ref, wk_ref, wvt_ref, wqt_ref,
                k_ref, vt_ref, qt_ref):
    n_heads = qt_ref.shape[1]
    t = h_ref.shape[0]
    xkv = (_rms_scale(hp_ref[...]) * gkv_ref[...]).astype(BF16)
    xq = (_rms_scale(h_ref[...]) * gq_ref[...]).astype(BF16)
    k_ref[...] = jnp.dot(xkv, wk_ref[...], preferred_element_type=F32).astype(BF16)
    vt_ref[0] = lax.dot_general(wvt_ref[...], xkv, _NT,
                                preferred_element_type=F32).astype(BF16)
    qt = (lax.dot_general(wqt_ref[...], xq, _NT, preferred_element_type=F32)
          * (0.5 * HEAD_DIM ** -0.5)).astype(BF16)
    zeros = jnp.zeros((HEAD_DIM, t), BF16)
    for hd in range(n_heads):
        q_h = qt[hd * HEAD_DIM:(hd + 1) * HEAD_DIM, :]
        if hd % 2 == 0:
            qt_ref[0, hd, 0:HEAD_DIM, :] = q_h
            qt_ref[0, hd, HEAD_DIM:2 * HEAD_DIM, :] = zeros
        else:
            qt_ref[0, hd, 0:HEAD_DIM, :] = zeros
            qt_ref[0, hd, HEAD_DIM:2 * HEAD_DIM, :] = q_h


def _key_permute(x, seq_len):
    n, d = x.shape
    kb = min(ATTN_BLOCK, seq_len)
    seg = kb // SUBLANES
    return x.reshape(n // kb, SUBLANES, seg, d).swapaxes(1, 2).reshape(n, d)


def _kvq_proj(h, g_kv, g_q, w_kv, w_q, batch, seq_len):
    n, d = h.shape
    n_heads = d // HEAD_DIM
    t = min(PROJ_TOKENS, seq_len)
    tiles = seq_len // t
    assert seq_len % t == 0 and t % min(ATTN_BLOCK, seq_len) == 0 and n_heads % 2 == 0
    w_k, w_v = w_kv[:, :d], w_kv[:, d:]
    return pl.pallas_call(
        _kvq_kernel,
        grid=(n // t,),
        in_specs=[
            pl.BlockSpec((t, d), lambda i: (i, 0)),
            pl.BlockSpec((t, d), lambda i: (i, 0)),
            pl.BlockSpec((1, d), lambda i: (0, 0)),
            pl.BlockSpec((1, d), lambda i: (0, 0)),
            pl.BlockSpec((d, d), lambda i: (0, 0)),
            pl.BlockSpec((d, d), lambda i: (0, 0)),
            pl.BlockSpec((d, d), lambda i: (0, 0)),
        ],
        out_specs=[
            pl.BlockSpec((t, d), lambda i: (i, 0)),
            pl.BlockSpec((1, d, t), lambda i: (i // tiles, 0, i % tiles)),
            pl.BlockSpec((1, n_heads, 2 * HEAD_DIM, t), lambda i: (i // tiles, 0, 0, i % tiles)),
        ],
        out_shape=[
            jax.ShapeDtypeStruct((n, d), BF16),
            jax.ShapeDtypeStruct((batch, d, seq_len), BF16),
            jax.ShapeDtypeStruct((batch, n_heads, 2 * HEAD_DIM, seq_len), BF16),
        ],
        compiler_params=_params("arbitrary"),
        name="kvq_proj",
    )(h, _key_permute(h, seq_len), g_kv.reshape(1, d), g_q.reshape(1, d),
      w_k.astype(BF16), w_v.T.astype(BF16), w_q.T.astype(BF16))


def _attn_kernel(qt_ref, k_ref, vt_ref, bias_ref, o_ref, *, blk, tiles):
    seg = blk // SUBLANES
    width = 2 * blk
    first_tile = pl.program_id(2) * tiles
    qts = [jnp.concatenate([qt_ref[0, 0, :, u * blk:(u + 1) * blk],
                            qt_ref[0, 1, :, u * blk:(u + 1) * blk]], axis=1)
           for u in range(tiles)]

    def block(u, kb, carry, acc0, acc1, masked):
        k0 = pl.multiple_of(kb * blk, blk)
        zh = jnp.dot(k_ref[0, pl.ds(k0, blk), :], qts[u], preferred_element_type=F32)
        if masked:
            zh = zh + bias_ref[...]
        ht = 0.5 * jnp.tanh(zh)
        beta = 0.5 + ht
        keep = 0.5 - ht
        run = jnp.ones((SUBLANES, width), F32)
        parts = [None] * seg
        for rr in reversed(range(seg)):
            parts[rr] = run
            run = run * keep[rr * SUBLANES:(rr + 1) * SUBLANES, :]
        offs = [None] * SUBLANES
        c = carry
        for s in reversed(range(SUBLANES)):
            offs[s] = c
            c = c * run[s:s + 1, :]
        off = jnp.concatenate(offs, axis=0)
        w = jnp.concatenate(
            [beta[rr * SUBLANES:(rr + 1) * SUBLANES, :] * (parts[rr] * off)
             for rr in range(seg)], axis=0).astype(BF16)
        vt = vt_ref[0, :, pl.ds(k0, blk)]
        acc0 = acc0 + jnp.dot(vt[0:HEAD_DIM, :], w[:, 0:blk], preferred_element_type=F32)
        acc1 = acc1 + jnp.dot(vt[HEAD_DIM:, :], w[:, blk:], preferred_element_type=F32)
        return c, acc0, acc1

    ones = jnp.ones((1, width), F32)
    zero = jnp.zeros((HEAD_DIM, blk), F32)
    state = (jnp.int32(0),)
    for u in range(tiles):
        state = state + block(u, first_tile + u, ones, zero, zero, True)

    def live_carries(st):
        return [jnp.where(st[0] < first_tile + u, st[1 + 3 * u], 0.0) for u in range(tiles)]

    def cond(st):
        return jnp.max(functools.reduce(jnp.maximum, live_carries(st))) > 0.0

    def body(st):
        new = (st[0] + 1,)
        for u, carry in enumerate(live_carries(st)):
            kb = jnp.maximum(first_tile + u - 1 - st[0], 0)
            new = new + block(u, kb, carry, st[2 + 3 * u], st[3 + 3 * u], False)
        return new

    state = lax.while_loop(cond, body, state)
    for u in range(tiles):
        o_ref[0, 0:HEAD_DIM, u * blk:(u + 1) * blk] = state[2 + 3 * u]
        o_ref[0, HEAD_DIM:, u * blk:(u + 1) * blk] = state[3 + 3 * u]


def _causal_bias(blk):
    seg = blk // SUBLANES
    row = jnp.arange(blk, dtype=jnp.int32)[:, None]
    key_local = (row % SUBLANES) * seg + row // SUBLANES
    query_local = jnp.arange(2 * blk, dtype=jnp.int32)[None, :] % blk
    return jnp.where(key_local < query_local, 0.0, -1e30).astype(F32)


def _stick_breaking_attention(qt_pad, k_perm, vt_perm, seq_len):
    batch, n_heads = qt_pad.shape[0], qt_pad.shape[1]
    d = n_heads * HEAD_DIM
    blk = min(ATTN_BLOCK, seq_len)
    tiles = min(ATTN_TILES, seq_len // blk)
    assert seq_len % (blk * tiles) == 0
    k3 = k_perm.reshape(batch, seq_len, d)
    return pl.pallas_call(
        functools.partial(_attn_kernel, blk=blk, tiles=tiles),
        grid=(batch, n_heads // 2, seq_len // (blk * tiles)),
        in_specs=[
            pl.BlockSpec((1, 2, 2 * HEAD_DIM, blk * tiles), lambda b, p, q: (b, p, 0, q)),
            pl.BlockSpec((1, seq_len, 2 * HEAD_DIM), lambda b, p, q: (b, 0, p)),
            pl.BlockSpec((1, 2 * HEAD_DIM, seq_len), lambda b, p, q: (b, p, 0)),
            pl.BlockSpec((blk, 2 * blk), lambda b, p, q: (0, 0)),
        ],
        out_specs=pl.BlockSpec((1, 2 * HEAD_DIM, blk * tiles), lambda b, p, q: (b, p, q)),
        out_shape=jax.ShapeDtypeStruct((batch, d, seq_len), F32),
        compiler_params=_params("arbitrary", "arbitrary", "arbitrary"),
        name="stick_breaking_attention",
    )(qt_pad, k3, vt_perm, _causal_bias(blk))


def _out_proj_kernel(ot_ref, w_ref, h_ref, o_ref):
    a = ot_ref[0].T.astype(BF16)
    o_ref[...] = h_ref[...] + jnp.dot(a, w_ref[...], preferred_element_type=F32)


def _out_proj(ot, w_o, h, seq_len):
    n, d = h.shape
    t = min(PROJ_TOKENS, seq_len)
    tiles = seq_len // t
    return pl.pallas_call(
        _out_proj_kernel,
        grid=(n // t,),
        in_specs=[
            pl.BlockSpec((1, d, t), lambda i: (i // tiles, 0, i % tiles)),
            pl.BlockSpec((d, d), lambda i: (0, 0)),
            pl.BlockSpec((t, d), lambda i: (i, 0)),
        ],
        out_specs=pl.BlockSpec((t, d), lambda i: (i, 0)),
        out_shape=jax.ShapeDtypeStruct((n, d), F32),
        compiler_params=_params("arbitrary"),
        name="attn_out_proj",
    )(ot, w_o.astype(BF16), h)


def kernel(x, norm_mix, norm_ffn, conv_w_in, conv_kernel, conv_w_out, norm_kv, w_kv,
           attn_w_q, attn_w_o, peer_w_query, peer_subkeys, peer_u, peer_v, norm_final):
    batch, seq_len, d = x.shape
    assert norm_mix.shape[0] == 2 and conv_w_in.shape[0] == 1 and attn_w_q.shape[0] == 1
    h = x.reshape(batch * seq_len, d)
    h = _conv_mixer(h, norm_mix[0], conv_w_in[0], conv_kernel[0], conv_w_out[0], seq_len)
    h = _peer_ffn(h, norm_ffn[0], peer_w_query[0], peer_subkeys[0], peer_u[0], peer_v[0],
                  norm_final, final_norm=False)
    k_perm, vt_perm, qt_pad = _kvq_proj(h, norm_kv, norm_mix[1], w_kv, attn_w_q[0],
                                        batch, seq_len)
    ot = _stick_breaking_attention(qt_pad, k_perm, vt_perm, seq_len)
    h = _out_proj(ot, attn_w_o[0], h, seq_len)
    h = _peer_ffn(h, norm_ffn[1], peer_w_query[1], peer_subkeys[1], peer_u[1], peer_v[1],
                  norm_final, final_norm=True)
    return h.reshape(batch, seq_len, d)
```

```python
import functools
import math

import jax
import jax.numpy as jnp
from jax import lax
from jax.experimental import pallas as pl
from jax.experimental.pallas import tpu as pltpu

F32 = jnp.float32
BF16 = jnp.bfloat16

RMS_EPS = 1e-6
CONV_WIDTH = 3
HEAD_DIM = 64
PEER_HEADS = 8
PEER_N_KEYS = 128
PEER_KEY_DIM = 128
PEER_TOPK = 16

SUBLANES = 8
LANES = 128
VMEM_LIMIT_BYTES = 56 * 1024 * 1024

CONV_TOKENS = 512
TOPK_TOKENS = SUBLANES * LANES
SCORE_PITCH = PEER_N_KEYS + SUBLANES
TOPK_CHAINS = 4
GATE_TOKENS = LANES
GATE_UNROLL = 128
EXPERT_TOKENS = 1024
EXPERT_ROWS = 2 * SUBLANES
EXPERT_BLOCK = EXPERT_ROWS * PEER_N_KEYS
N_EXPERT_BLOCKS = PEER_N_KEYS // EXPERT_ROWS
EXPERT_CHUNK = 2 * PEER_N_KEYS
PROJ_TOKENS = 512
ATTN_BLOCK = 256
ATTN_TILES = 4

_NT = (((1,), (1,)), ((), ()))


def _params(*sem):
    return pltpu.CompilerParams(dimension_semantics=sem,
                                vmem_limit_bytes=VMEM_LIMIT_BYTES)


def _rms_scale(x):
    return x * lax.rsqrt(jnp.mean(x * x, axis=-1, keepdims=True) + RMS_EPS)


def _conv_mixer_kernel(x_ref, g_ref, win_ref, ck_ref, wout_ref, o_ref, ubuf_ref,
                       *, tiles_per_seq):
    t = x_ref.shape[0]
    d = x_ref.shape[1]
    halo = SUBLANES
    first = (pl.program_id(0) % tiles_per_seq) == 0

    @pl.when(first)
    def _():
        ubuf_ref[0:halo, :] = jnp.zeros((halo, d), F32)

    @pl.when(jnp.logical_not(first))
    def _():
        ubuf_ref[0:halo, :] = ubuf_ref[t:t + halo, :]

    x = x_ref[...]
    xb = (_rms_scale(x) * g_ref[...]).astype(BF16)
    bg = jnp.dot(xb, win_ref[:, 0:d], preferred_element_type=F32)
    cg = jnp.dot(xb, win_ref[:, d:2 * d], preferred_element_type=F32)
    hv = jnp.dot(xb, win_ref[:, 2 * d:3 * d], preferred_element_type=F32)
    u = cg * hv
    ubuf_ref[halo:halo + t, :] = u
    conv = ubuf_ref[halo - 2:halo - 2 + t, :] * ck_ref[0:1, :]
    conv = conv + ubuf_ref[halo - 1:halo - 1 + t, :] * ck_ref[1:2, :]
    conv = conv + u * ck_ref[2:3, :]
    y = (bg * conv).astype(BF16)
    o_ref[...] = x + jnp.dot(y, wout_ref[...], preferred_element_type=F32)


def _conv_mixer(h, gain, w_in, conv_k, w_out, seq_len):
    n, d = h.shape
    t = min(CONV_TOKENS, seq_len)
    assert seq_len % t == 0 and conv_k.shape[0] == CONV_WIDTH
    return pl.pallas_call(
        functools.partial(_conv_mixer_kernel, tiles_per_seq=seq_len // t),
        grid=(n // t,),
        in_specs=[
            pl.BlockSpec((t, d), lambda i: (i, 0)),
            pl.BlockSpec((1, d), lambda i: (0, 0)),
            pl.BlockSpec((d, 3 * d), lambda i: (0, 0)),
            pl.BlockSpec((CONV_WIDTH, d), lambda i: (0, 0)),
            pl.BlockSpec((d, d), lambda i: (0, 0)),
        ],
        out_specs=pl.BlockSpec((t, d), lambda i: (i, 0)),
        out_shape=jax.ShapeDtypeStruct((n, d), F32),
        scratch_shapes=[pltpu.VMEM((t + 2 * SUBLANES, d), F32)],
        compiler_params=_params("arbitrary"),
        name="conv_mixer",
    )(h, gain.reshape(1, d), w_in.astype(BF16), conv_k, w_out.astype(BF16))


def _top_planes(planes, ids, k):
    planes = list(planes)
    chunk = -(-len(planes) // TOPK_CHAINS)
    vals, picked = [], []
    for it in range(k):
        best = best_id = None
        for lo in range(0, len(planes), chunk):
            part = planes[lo]
            part_id = jnp.full(part.shape, ids[lo], F32)
            for plane, plane_id in zip(planes[lo + 1:lo + chunk], ids[lo + 1:lo + chunk]):
                better = plane > part
                part = jnp.where(better, plane, part)
                part_id = jnp.where(better, plane_id, part_id)
            if best is None:
                best, best_id = part, part_id
            else:
                better = part > best
                best = jnp.where(better, part, best)
                best_id = jnp.where(better, part_id, best_id)
        vals.append(best)
        picked.append(best_id)
        if it + 1 < k:
            planes = [jnp.where(best_id == plane_id, -jnp.inf, plane)
                      for plane, plane_id in zip(planes, ids)]
    return vals, picked


def _sort_network(n):
    pairs = []

    def merge(lo, count, stride):
        step = stride * 2
        if step < count:
            merge(lo, count, step)
            merge(lo + stride, count, step)
            pairs.extend((i, i + stride) for i in range(lo + stride, lo + count - stride, step))
        else:
            pairs.append((lo, lo + stride))

    def sort(lo, count):
        if count > 1:
            sort(lo, count // 2)
            sort(lo + count // 2, count // 2)
            merge(lo, count, 1)

    sort(0, n)
    return pairs


def _order(vals, ids, i, j):
    swap = vals[j] > vals[i]
    vals[i], vals[j] = jnp.where(swap, vals[j], vals[i]), jnp.where(swap, vals[i], vals[j])
    ids[i], ids[j] = jnp.where(swap, ids[j], ids[i]), jnp.where(swap, ids[i], ids[j])


def _top_planes_sorted(planes, ids, k):
    assert len(planes) % k == 0 and k & (k - 1) == 0
    network = _sort_network(k)
    groups = []
    for lo in range(0, len(planes), k):
        vals = list(planes[lo:lo + k])
        gids = [jnp.full(vals[0].shape, i, F32) for i in ids[lo:lo + k]]
        for i, j in network:
            _order(vals, gids, i, j)
        groups.append((vals, gids))
    return _merge_runs(groups, planes, k)


def _bitonic_merge(vals, ids):
    dist = len(vals) // 2
    while dist:
        for i in range(len(vals)):
            if i & dist == 0:
                _order(vals, ids, i, i + dist)
        dist //= 2


def _merge_runs(runs, planes, k):
    while len(runs) > 1:
        merged = [runs[-1]] if len(runs) % 2 else []
        for (va, ia), (vb, ib) in zip(runs[0:-1:2], runs[1::2]):
            take_b = [vb[k - 1 - i] > va[i] for i in range(k)]
            vals = [jnp.where(t, vb[k - 1 - i], va[i]) for i, t in enumerate(take_b)]
            ids = [jnp.where(t, ib[k - 1 - i], ia[i]) for i, t in enumerate(take_b)]
            _bitonic_merge(vals, ids)
            merged.append((vals, ids))
        runs = merged
    vals, ids = runs[0]
    tied = jnp.zeros(vals[0].shape, F32)
    for i in range(k - 1):
        tied = jnp.where(vals[i] > vals[i + 1], tied, 1.0)
    reach = jnp.zeros(vals[0].shape, F32)
    for plane in planes:
        reach = reach + jnp.where(plane >= vals[k - 1], 1.0, 0.0)
    tied = jnp.where(reach == float(k), tied, 1.0)
    return vals, ids, tied


def _top_sums_sorted(cands, cand_ids, k):
    half = k // 2
    as_run = lambda lo, hi: (list(cands[lo:hi]), [jnp.full(cands[0].shape, i, F32)
                                                  for i in cand_ids[lo:hi]])
    runs = [as_run(0, k)]
    shorts = [as_run(lo, lo + half) for lo in range(k, len(cands), half)]
    assert len(shorts) % 2 == 0
    for (va, ia), (vb, ib) in zip(shorts[0::2], shorts[1::2]):
        vals, ids = va + vb[::-1], ia + ib[::-1]
        _bitonic_merge(vals, ids)
        runs.append((vals, ids))
    return _merge_runs(runs, cands, k)


def _lookup(table, sel):
    out = jnp.zeros(sel.shape, F32)
    for a, plane in enumerate(table):
        out = out + jnp.where(sel == float(a), plane, 0.0)
    return out


def _score_planes(scores, scratch_ref):
    for g in range(SUBLANES):
        scratch_ref[g * SCORE_PITCH:g * SCORE_PITCH + PEER_N_KEYS, :] = (
            scores[:, g * LANES:(g + 1) * LANES])
    return [scratch_ref[pl.ds(n, SUBLANES, stride=SCORE_PITCH), :] for n in range(PEER_N_KEYS)]


def _peer_topk_kernel(h_ref, g_ref, wqt_ref, sk_ref, xn_ref, picks_ref, s0_ref, s1_ref):
    k = PEER_TOPK
    half = k // 2
    xn_ref[...] = (_rms_scale(h_ref[...]) * g_ref[...]).astype(BF16)
    key_ids = [float(n) for n in range(PEER_N_KEYS)]
    cand_ab = ([(0, b) for b in range(k)]
               + [(a, b) for a in range(1, half) for b in range(half)]
               + [(a, 0) for a in range(half, k)])
    cand_ids = [float(a * k + b) for a, b in cand_ab]

    def head_body(hd, carry):
        q0 = pl.multiple_of(hd * 2 * PEER_KEY_DIM, 2 * PEER_KEY_DIM)
        qt = lax.dot_general(wqt_ref[pl.ds(q0, 2 * PEER_KEY_DIM), :], xn_ref[...], _NT,
                             preferred_element_type=F32).astype(BF16)
        planes = []
        for p, scratch_ref in enumerate((s0_ref, s1_ref)):
            scores = jnp.dot(sk_ref[hd, p], qt[p * PEER_KEY_DIM:(p + 1) * PEER_KEY_DIM, :],
                             preferred_element_type=F32)
            planes.append(_score_planes(scores, scratch_ref))
        v1, i1, tied1 = _top_planes_sorted(planes[0], key_ids, k)
        v2, i2, tied2 = _top_planes_sorted(planes[1], key_ids, k)

        def exact_scan():
            return _top_planes(planes[0], key_ids, k) + _top_planes(planes[1], key_ids, k)

        v1, i1, v2, i2 = lax.cond(jnp.max(jnp.maximum(tied1, tied2)) > 0.0,
                                  exact_scan, lambda: (v1, i1, v2, i2))
        cands = [v1[a] + v2[b] for a, b in cand_ab]
        top_s, top_c, tied = _top_sums_sorted(cands, cand_ids, k)
        top_s, top_c = lax.cond(jnp.max(tied) > 0.0,
                                lambda: _top_planes(cands, cand_ids, k),
                                lambda: (top_s, top_c))
        denom = None
        gates = []
        for j in range(k):
            gates.append(jnp.exp(top_s[j] - top_s[0]))
            denom = gates[j] if denom is None else denom + gates[j]
        for j in range(k):
            a_sel = jnp.floor(top_c[j] * (1.0 / k))
            b_sel = top_c[j] - a_sel * k
            row0 = pl.multiple_of((hd * k + j) * SUBLANES, SUBLANES)
            picks_ref[0, 0, pl.ds(row0, SUBLANES), :] = _lookup(i1, a_sel)
            picks_ref[0, 1, pl.ds(row0, SUBLANES), :] = _lookup(i2, b_sel)
            picks_ref[0, 2, pl.ds(row0, SUBLANES), :] = gates[j] / denom
        return carry

    lax.fori_loop(0, PEER_HEADS, head_body, 0)


def _peer_topk(h, gain, w_query, subkeys):
    n, d = h.shape
    t = TOPK_TOKENS
    qdim = PEER_HEADS * 2 * PEER_KEY_DIM
    n_picks = PEER_HEADS * PEER_TOPK
    assert subkeys.shape == (PEER_HEADS, 2, PEER_N_KEYS, PEER_KEY_DIM)
    assert w_query.shape == (d, qdim) and n % t == 0
    return pl.pallas_call(
        _peer_topk_kernel,
        grid=(n // t,),
        in_specs=[
            pl.BlockSpec((t, d), lambda i: (i, 0)),
            pl.BlockSpec((1, d), lambda i: (0, 0)),
            pl.BlockSpec((qdim, d), lambda i: (0, 0)),
            pl.BlockSpec((PEER_HEADS, 2, PEER_N_KEYS, PEER_KEY_DIM), lambda i: (0, 0, 0, 0)),
        ],
        out_specs=[
            pl.BlockSpec((t, d), lambda i: (i, 0)),
            pl.BlockSpec((1, 3, n_picks * SUBLANES, LANES), lambda i: (i, 0, 0, 0)),
        ],
        out_shape=[
            jax.ShapeDtypeStruct((n, d), BF16),
            jax.ShapeDtypeStruct((n // t, 3, n_picks * SUBLANES, LANES), F32),
        ],
        scratch_shapes=[pltpu.VMEM((SUBLANES * SCORE_PITCH, LANES), F32)] * 2,
        compiler_params=_params("arbitrary"),
        name="peer_topk",
    )(h, gain.reshape(1, d), w_query.T.astype(BF16), subkeys.astype(BF16))


def _unpack_bf16_pair(words, which):
    if which == 0:
        return pltpu.bitcast(lax.shift_left(words, jnp.uint32(16)), F32)
    return pltpu.bitcast(words & jnp.uint32(0xFFFF0000), F32)


def _gate_build_kernel(picks_ref, gate_ref, i1t_s, i2t_s, gtt_s):
    n_picks = picks_ref.shape[2] // SUBLANES
    group = pl.program_id(1)
    for which, dst in enumerate((i1t_s, i2t_s, gtt_s)):
        dst[...] = picks_ref[0, which, pl.ds(group, n_picks, stride=SUBLANES), :].T
    sub_id = lax.broadcasted_iota(jnp.int32, (PEER_N_KEYS, n_picks), 0).astype(F32)

    def token_body(c, carry):
        i1r = i1t_s[pl.ds(c, 1), :]
        i2r = i2t_s[pl.ds(c, 1), :]
        gr = gtt_s[pl.ds(c, 1), :]
        a = jnp.where(sub_id == i1r, gr, 0.0).astype(BF16)
        b = jnp.where(sub_id == i2r, 1.0, 0.0).astype(BF16)
        g = lax.dot_general(a, b, _NT, preferred_element_type=F32).astype(BF16)
        for blk in range(N_EXPERT_BLOCKS):
            gate_ref[blk, c] = pltpu.bitcast(g[blk * EXPERT_ROWS:(blk + 1) * EXPERT_ROWS, :],
                                             jnp.uint32)
        return carry

    lax.fori_loop(0, GATE_TOKENS, token_body, 0, unroll=GATE_UNROLL)


def _gate_build(picks):
    tiles, _, pick_rows, _ = picks.shape
    n = tiles * TOPK_TOKENS
    n_picks = pick_rows // SUBLANES
    return pl.pallas_call(
        _gate_build_kernel,
        grid=(tiles, SUBLANES),
        in_specs=[pl.BlockSpec((1, 3, pick_rows, LANES), lambda i, g: (i, 0, 0, 0))],
        out_specs=pl.BlockSpec((N_EXPERT_BLOCKS, GATE_TOKENS, SUBLANES, PEER_N_KEYS),
                               lambda i, g: (0, i * SUBLANES + g, 0, 0)),
        out_shape=jax.ShapeDtypeStruct((N_EXPERT_BLOCKS, n, SUBLANES, PEER_N_KEYS), jnp.uint32),
        scratch_shapes=[pltpu.VMEM((GATE_TOKENS, n_picks), F32)] * 3,
        compiler_params=_params("arbitrary", "arbitrary"),
        name="peer_gate_build",
    )(picks)


def _peer_expert_kernel(xn_ref, gate_ref, u_ref, v_ref, h_ref, fg_ref, o_ref, coef_ref,
                        *, final_norm):
    e = pl.program_id(1)
    t = xn_ref.shape[0]

    @pl.when(e == 0)
    def _():
        o_ref[...] = h_ref[...]

    xn = xn_ref[...]
    per_chunk = EXPERT_CHUNK // PEER_N_KEYS
    for q in range(EXPERT_BLOCK // EXPERT_CHUNK):
        cols = slice(q * EXPERT_CHUNK, (q + 1) * EXPERT_CHUNK)
        act = lax.dot_general(xn, u_ref[cols, :], _NT, preferred_element_type=F32)
        gelu = 0.5 * act * (1.0 + lax.erf(act * (1.0 / math.sqrt(2.0))))
        assert per_chunk == 2
        words = gate_ref[0, pl.ds(q, t, stride=SUBLANES), :]
        gate = jnp.concatenate([_unpack_bf16_pair(words, 0), _unpack_bf16_pair(words, 1)], axis=1)
        coef_ref[:, cols] = (gate * gelu).astype(BF16)
    o_ref[...] += jnp.dot(coef_ref[...], v_ref[...], preferred_element_type=F32)

    if final_norm:
        @pl.when(e == pl.num_programs(1) - 1)
        def _():
            o_ref[...] = _rms_scale(o_ref[...]) * fg_ref[...]


def _peer_experts(xn, gate, u_tbl, v_tbl, h, final_gain, final_norm):
    n, d = h.shape
    t = min(EXPERT_TOKENS, n)
    assert n % t == 0 and u_tbl.shape == (N_EXPERT_BLOCKS * EXPERT_BLOCK, d)
    gate2 = gate.reshape(gate.shape[0], n * SUBLANES, PEER_N_KEYS)
    return pl.pallas_call(
        functools.partial(_peer_expert_kernel, final_norm=final_norm),
        grid=(n // t, N_EXPERT_BLOCKS),
        in_specs=[
            pl.BlockSpec((t, d), lambda i, e: (i, 0)),
            pl.BlockSpec((1, t * SUBLANES, PEER_N_KEYS), lambda i, e: (e, i, 0)),
            pl.BlockSpec((EXPERT_BLOCK, d), lambda i, e: (e, 0)),
            pl.BlockSpec((EXPERT_BLOCK, d), lambda i, e: (e, 0)),
            pl.BlockSpec((t, d), lambda i, e: (i, 0)),
            pl.BlockSpec((1, d), lambda i, e: (0, 0)),
        ],
        out_specs=pl.BlockSpec((t, d), lambda i, e: (i, 0)),
        out_shape=jax.ShapeDtypeStruct((n, d), F32),
        scratch_shapes=[pltpu.VMEM((t, EXPERT_BLOCK), BF16)],
        compiler_params=_params("arbitrary", "arbitrary"),
        name="peer_experts",
    )(xn, gate2, u_tbl.astype(BF16), v_tbl.astype(BF16), h, final_gain.reshape(1, d))


def _peer_ffn(h, gain, w_query, subkeys, u_tbl, v_tbl, final_gain, final_norm):
    xn, picks = _peer_topk(h, gain, w_query, subkeys)
    return _peer_experts(xn, _gate_build(picks), u_tbl, v_tbl, h, final_gain, final_norm)


def _kvq_kernel(h_ref, hp_ref, gkv_ref, gq_ref, wk_ref, wvt_ref, wqt_ref,
                k_ref, vt_ref, qt_ref):
    n_heads = qt_ref.shape[1]
    t = h_ref.shape[0]
    xkv = (_rms_scale(hp_ref[...]) * gkv_ref[...]).astype(BF16)
    xq = (_rms_scale(h_ref[...]) * gq_ref[...]).astype(BF16)
    k_ref[...] = jnp.dot(xkv, wk_ref[...], preferred_element_type=F32).astype(BF16)
    vt_ref[0] = lax.dot_general(wvt_ref[...], xkv, _NT,
                                preferred_element_type=F32).astype(BF16)
    qt = (lax.dot_general(wqt_ref[...], xq, _NT, preferred_element_type=F32)
          * (0.5 * HEAD_DIM ** -0.5)).astype(BF16)
    zeros = jnp.zeros((HEAD_DIM, t), BF16)
    for hd in range(n_heads):
        q_h = qt[hd * HEAD_DIM:(hd + 1) * HEAD_DIM, :]
        if hd % 2 == 0:
            qt_ref[0, hd, 0:HEAD_DIM, :] = q_h
            qt_ref[0, hd, HEAD_DIM:2 * HEAD_DIM, :] = zeros
        else:
            qt_ref[0, hd, 0:HEAD_DIM, :] = zeros
            qt_ref[0, hd, HEAD_DIM:2 * HEAD_DIM, :] = q_h


def _key_permute(x, seq_len):
    n, d = x.shape
    kb = min(ATTN_BLOCK, seq_len)
    seg = kb // SUBLANES
    return x.reshape(n // kb, SUBLANES, seg, d).swapaxes(1, 2).reshape(n, d)


def _kvq_proj(h, g_kv, g_q, w_kv, w_q, batch, seq_len):
    n, d = h.shape
    n_heads = d // HEAD_DIM
    t = min(PROJ_TOKENS, seq_len)
    tiles = seq_len // t
    assert seq_len % t == 0 and t % min(ATTN_BLOCK, seq_len) == 0 and n_heads % 2 == 0
    w_k, w_v = w_kv[:, :d], w_kv[:, d:]
    return pl.pallas_call(
        _kvq_kernel,
        grid=(n // t,),
        in_specs=[
            pl.BlockSpec((t, d), lambda i: (i, 0)),
            pl.BlockSpec((t, d), lambda i: (i, 0)),
            pl.BlockSpec((1, d), lambda i: (0, 0)),
            pl.BlockSpec((1, d), lambda i: (0, 0)),
            pl.BlockSpec((d, d), lambda i: (0, 0)),
            pl.BlockSpec((d, d), lambda i: (0, 0)),
            pl.BlockSpec((d, d), lambda i: (0, 0)),
        ],
        out_specs=[
            pl.BlockSpec((t, d), lambda i: (i, 0)),
            pl.BlockSpec((1, d, t), lambda i: (i // tiles, 0, i % tiles)),
            pl.BlockSpec((1, n_heads, 2 * HEAD_DIM, t), lambda i: (i // tiles, 0, 0, i % tiles)),
        ],
        out_shape=[
            jax.ShapeDtypeStruct((n, d), BF16),
            jax.ShapeDtypeStruct((batch, d, seq_len), BF16),
            jax.ShapeDtypeStruct((batch, n_heads, 2 * HEAD_DIM, seq_len), BF16),
        ],
        compiler_params=_params("arbitrary"),
        name="kvq_proj",
    )(h, _key_permute(h, seq_len), g_kv.reshape(1, d), g_q.reshape(1, d),
      w_k.astype(BF16), w_v.T.astype(BF16), w_q.T.astype(BF16))


def _attn_kernel(qt_ref, k_ref, vt_ref, bias_ref, o_ref, *, blk, tiles):
    seg = blk // SUBLANES
    width = 2 * blk
    first_tile = pl.program_id(2) * tiles
    qts = [jnp.concatenate([qt_ref[0, 0, :, u * blk:(u + 1) * blk],
                            qt_ref[0, 1, :, u * blk:(u + 1) * blk]], axis=1)
           for u in range(tiles)]

    def block(u, kb, carry, acc0, acc1, masked):
        k0 = pl.multiple_of(kb * blk, blk)
        zh = jnp.dot(k_ref[0, pl.ds(k0, blk), :], qts[u], preferred_element_type=F32)
        if masked:
            zh = zh + bias_ref[...]
        ht = 0.5 * jnp.tanh(zh)
        beta = 0.5 + ht
        keep = 0.5 - ht
        run = jnp.ones((SUBLANES, width), F32)
        parts = [None] * seg
        for rr in reversed(range(seg)):
            parts[rr] = run
            run = run * keep[rr * SUBLANES:(rr + 1) * SUBLANES, :]
        offs = [None] * SUBLANES
        c = carry
        for s in reversed(range(SUBLANES)):
            offs[s] = c
            c = c * run[s:s + 1, :]
        off = jnp.concatenate(offs, axis=0)
        w = jnp.concatenate(
            [beta[rr * SUBLANES:(rr + 1) * SUBLANES, :] * (parts[rr] * off)
             for rr in range(seg)], axis=0).astype(BF16)
        vt = vt_ref[0, :, pl.ds(k0, blk)]
        acc0 = acc0 + jnp.dot(vt[0:HEAD_DIM, :], w[:, 0:blk], preferred_element_type=F32)
        acc1 = acc1 + jnp.dot(vt[HEAD_DIM:, :], w[:, blk:], preferred_element_type=F32)
        return c, acc0, acc1

    ones = jnp.ones((1, width), F32)
    zero = jnp.zeros((HEAD_DIM, blk), F32)
    state = (jnp.int32(0),)
    for u in range(tiles):
        state = state + block(u, first_tile + u, ones, zero, zero, True)

    def live_carries(st):
        return [jnp.where(st[0] < first_tile + u, st[1 + 3 * u], 0.0) for u in range(tiles)]

    def cond(st):
        return jnp.max(functools.reduce(jnp.maximum, live_carries(st))) > 0.0

    def body(st):
        new = (st[0] + 1,)
        for u, carry in enumerate(live_carries(st)):
            kb = jnp.maximum(first_tile + u - 1 - st[0], 0)
            new = new + block(u, kb, carry, st[2 + 3 * u], st[3 + 3 * u], False)
        return new

    state = lax.while_loop(cond, body, state)
    for u in range(tiles):
        o_ref[0, 0:HEAD_DIM, u * blk:(u + 1) * blk] = state[2 + 3 * u]
        o_ref[0, HEAD_DIM:, u * blk:(u + 1) * blk] = state[3 + 3 * u]


def _causal_bias(blk):
    seg = blk // SUBLANES
    row = jnp.arange(blk, dtype=jnp.int32)[:, None]
    key_local = (row % SUBLANES) * seg + row // SUBLANES
    query_local = jnp.arange(2 * blk, dtype=jnp.int32)[None, :] % blk
    return jnp.where(key_local < query_local, 0.0, -1e30).astype(F32)


def _stick_breaking_attention(qt_pad, k_perm, vt_perm, seq_len):
    batch, n_heads = qt_pad.shape[0], qt_pad.shape[1]
    d = n_heads * HEAD_DIM
    blk = min(ATTN_BLOCK, seq_len)
    tiles = min(ATTN_TILES, seq_len // blk)
    assert seq_len % (blk * tiles) == 0
    k3 = k_perm.reshape(batch, seq_len, d)
    return pl.pallas_call(
        functools.partial(_attn_kernel, blk=blk, tiles=tiles),
        grid=(batch, n_heads // 2, seq_len // (blk * tiles)),
        in_specs=[
            pl.BlockSpec((1, 2, 2 * HEAD_DIM, blk * tiles), lambda b, p, q: (b, p, 0, q)),
            pl.BlockSpec((1, seq_len, 2 * HEAD_DIM), lambda b, p, q: (b, 0, p)),
            pl.BlockSpec((1, 2 * HEAD_DIM, seq_len), lambda b, p, q: (b, p, 0)),
            pl.BlockSpec((blk, 2 * blk), lambda b, p, q: (0, 0)),
        ],
        out_specs=pl.BlockSpec((1, 2 * HEAD_DIM, blk * tiles), lambda b, p, q: (b, p, q)),
        out_shape=jax.ShapeDtypeStruct((batch, d, seq_len), F32),
        compiler_params=_params("arbitrary", "arbitrary", "arbitrary"),
        name="stick_breaking_attention",
    )(qt_pad, k3, vt_perm, _causal_bias(blk))


def _out_proj_kernel(ot_ref, w_ref, h_ref, o_ref):
    a = ot_ref[0].T.astype(BF16)
    o_ref[...] = h_ref[...] + jnp.dot(a, w_ref[...], preferred_element_type=F32)


def _out_proj(ot, w_o, h, seq_len):
    n, d = h.shape
    t = min(PROJ_TOKENS, seq_len)
    tiles = seq_len // t
    return pl.pallas_call(
        _out_proj_kernel,
        grid=(n // t,),
        in_specs=[
            pl.BlockSpec((1, d, t), lambda i: (i // tiles, 0, i % tiles)),
            pl.BlockSpec((d, d), lambda i: (0, 0)),
            pl.BlockSpec((t, d), lambda i: (i, 0)),
        ],
        out_specs=pl.BlockSpec((t, d), lambda i: (i, 0)),
        out_shape=jax.ShapeDtypeStruct((n, d), F32),
        compiler_params=_params("arbitrary"),
        name="attn_out_proj",
    )(ot, w_o.astype(BF16), h)


def kernel(x, norm_mix, norm_ffn, conv_w_in, conv_kernel, conv_w_out, norm_kv, w_kv,
           attn_w_q, attn_w_o, peer_w_query, peer_subkeys, peer_u, peer_v, norm_final):
    batch, seq_len, d = x.shape
    assert norm_mix.shape[0] == 2 and conv_w_in.shape[0] == 1 and attn_w_q.shape[0] == 1
    h = x.reshape(batch * seq_len, d)
    h = _conv_mixer(h, norm_mix[0], conv_w_in[0], conv_kernel[0], conv_w_out[0], seq_len)
    h = _peer_ffn(h, norm_ffn[0], peer_w_query[0], peer_subkeys[0], peer_u[0], peer_v[0],
                  norm_final, final_norm=False)
    k_perm, vt_perm, qt_pad = _kvq_proj(h, norm_kv, norm_mix[1], w_kv, attn_w_q[0],
                                        batch, seq_len)
    ot = _stick_breaking_attention(qt_pad, k_perm, vt_perm, seq_len)
    h = _out_proj(ot, attn_w_o[0], h, seq_len)
    h = _peer_ffn(h, norm_ffn[1], peer_w_query[1], peer_subkeys[1], peer_u[1], peer_v[1],
                  norm_final, final_norm=True)
    return h.reshape(batch, seq_len, d)
```

```python
import functools
import math

import jax
import jax.numpy as jnp
from jax import lax
from jax.experimental import pallas as pl
from jax.experimental.pallas import tpu as pltpu

F32 = jnp.float32
BF16 = jnp.bfloat16

RMS_EPS = 1e-6
CONV_WIDTH = 3
HEAD_DIM = 64
PEER_HEADS = 8
PEER_N_KEYS = 128
PEER_KEY_DIM = 128
PEER_TOPK = 16

SUBLANES = 8
LANES = 128
VMEM_LIMIT_BYTES = 56 * 1024 * 1024

CONV_TOKENS = 512
TOPK_TOKENS = SUBLANES * LANES
SCORE_PITCH = PEER_N_KEYS + SUBLANES
TOPK_CHAINS = 4
GATE_TOKENS = LANES
GATE_UNROLL = 128
EXPERT_TOKENS = 1024
EXPERT_ROWS = 2 * SUBLANES
EXPERT_BLOCK = EXPERT_ROWS * PEER_N_KEYS
N_EXPERT_BLOCKS = PEER_N_KEYS // EXPERT_ROWS
EXPERT_CHUNK = 2 * PEER_N_KEYS
PROJ_TOKENS = 512
ATTN_BLOCK = 128
ATTN_TILES = 8

_NT = (((1,), (1,)), ((), ()))


def _params(*sem):
    return pltpu.CompilerParams(dimension_semantics=sem,
                                vmem_limit_bytes=VMEM_LIMIT_BYTES)


def _rms_scale(x):
    return x * lax.rsqrt(jnp.mean(x * x, axis=-1, keepdims=True) + RMS_EPS)


def _conv_mixer_kernel(x_ref, g_ref, win_ref, ck_ref, wout_ref, o_ref, ubuf_ref,
                       *, tiles_per_seq):
    t = x_ref.shape[0]
    d = x_ref.shape[1]
    halo = SUBLANES
    first = (pl.program_id(0) % tiles_per_seq) == 0

    @pl.when(first)
    def _():
        ubuf_ref[0:halo, :] = jnp.zeros((halo, d), F32)

    @pl.when(jnp.logical_not(first))
    def _():
        ubuf_ref[0:halo, :] = ubuf_ref[t:t + halo, :]

    x = x_ref[...]
    xb = (_rms_scale(x) * g_ref[...]).astype(BF16)
    bg = jnp.dot(xb, win_ref[:, 0:d], preferred_element_type=F32)
    cg = jnp.dot(xb, win_ref[:, d:2 * d], preferred_element_type=F32)
    hv = jnp.dot(xb, win_ref[:, 2 * d:3 * d], preferred_element_type=F32)
    u = cg * hv
    ubuf_ref[halo:halo + t, :] = u
    conv = ubuf_ref[halo - 2:halo - 2 + t, :] * ck_ref[0:1, :]
    conv = conv + ubuf_ref[halo - 1:halo - 1 + t, :] * ck_ref[1:2, :]
    conv = conv + u * ck_ref[2:3, :]
    y = (bg * conv).astype(BF16)
    o_ref[...] = x + jnp.dot(y, wout_ref[...], preferred_element_type=F32)


def _conv_mixer(h, gain, w_in, conv_k, w_out, seq_len):
    n, d = h.shape
    t = min(CONV_TOKENS, seq_len)
    assert seq_len % t == 0 and conv_k.shape[0] == CONV_WIDTH
    return pl.pallas_call(
        functools.partial(_conv_mixer_kernel, tiles_per_seq=seq_len // t),
        grid=(n // t,),
        in_specs=[
            pl.BlockSpec((t, d), lambda i: (i, 0)),
            pl.BlockSpec((1, d), lambda i: (0, 0)),
            pl.BlockSpec((d, 3 * d), lambda i: (0, 0)),
            pl.BlockSpec((CONV_WIDTH, d), lambda i: (0, 0)),
            pl.BlockSpec((d, d), lambda i: (0, 0)),
        ],
        out_specs=pl.BlockSpec((t, d), lambda i: (i, 0)),
        out_shape=jax.ShapeDtypeStruct((n, d), F32),
        scratch_shapes=[pltpu.VMEM((t + 2 * SUBLANES, d), F32)],
        compiler_params=_params("arbitrary"),
        name="conv_mixer",
    )(h, gain.reshape(1, d), w_in.astype(BF16), conv_k, w_out.astype(BF16))


def _top_planes(planes, ids, k):
    planes = list(planes)
    chunk = -(-len(planes) // TOPK_CHAINS)
    vals, picked = [], []
    for it in range(k):
        best = best_id = None
        for lo in range(0, len(planes), chunk):
            part = planes[lo]
            part_id = jnp.full(part.shape, ids[lo], F32)
            for plane, plane_id in zip(planes[lo + 1:lo + chunk], ids[lo + 1:lo + chunk]):
                better = plane > part
                part = jnp.where(better, plane, part)
                part_id = jnp.where(better, plane_id, part_id)
            if best is None:
                best, best_id = part, part_id
            else:
                better = part > best
                best = jnp.where(better, part, best)
                best_id = jnp.where(better, part_id, best_id)
        vals.append(best)
        picked.append(best_id)
        if it + 1 < k:
            planes = [jnp.where(best_id == plane_id, -jnp.inf, plane)
                      for plane, plane_id in zip(planes, ids)]
    return vals, picked


def _sort_network(n):
    pairs = []

    def merge(lo, count, stride):
        step = stride * 2
        if step < count:
            merge(lo, count, step)
            merge(lo + stride, count, step)
            pairs.extend((i, i + stride) for i in range(lo + stride, lo + count - stride, step))
        else:
            pairs.append((lo, lo + stride))

    def sort(lo, count):
        if count > 1:
            sort(lo, count // 2)
            sort(lo + count // 2, count // 2)
            merge(lo, count, 1)

    sort(0, n)
    return pairs


def _order(vals, ids, i, j):
    swap = vals[j] > vals[i]
    vals[i], vals[j] = jnp.where(swap, vals[j], vals[i]), jnp.where(swap, vals[i], vals[j])
    ids[i], ids[j] = jnp.where(swap, ids[j], ids[i]), jnp.where(swap, ids[i], ids[j])


def _top_planes_sorted(planes, ids, k):
    assert len(planes) % k == 0 and k & (k - 1) == 0
    network = _sort_network(k)
    groups = []
    for lo in range(0, len(planes), k):
        vals = list(planes[lo:lo + k])
        gids = [jnp.full(vals[0].shape, i, F32) for i in ids[lo:lo + k]]
        for i, j in network:
            _order(vals, gids, i, j)
        groups.append((vals, gids))
    return _merge_runs(groups, planes, k)


def _bitonic_merge(vals, ids):
    dist = len(vals) // 2
    while dist:
        for i in range(len(vals)):
            if i & dist == 0:
                _order(vals, ids, i, i + dist)
        dist //= 2


def _merge_runs(runs, planes, k):
    while len(runs) > 1:
        merged = [runs[-1]] if len(runs) % 2 else []
        for (va, ia), (vb, ib) in zip(runs[0:-1:2], runs[1::2]):
            take_b = [vb[k - 1 - i] > va[i] for i in range(k)]
            vals = [jnp.where(t, vb[k - 1 - i], va[i]) for i, t in enumerate(take_b)]
            ids = [jnp.where(t, ib[k - 1 - i], ia[i]) for i, t in enumerate(take_b)]
            _bitonic_merge(vals, ids)
            merged.append((vals, ids))
        runs = merged
    vals, ids = runs[0]
    tied = jnp.zeros(vals[0].shape, F32)
    for i in range(k - 1):
        tied = jnp.where(vals[i] > vals[i + 1], tied, 1.0)
    reach = jnp.zeros(vals[0].shape, F32)
    for plane in planes:
        reach = reach + jnp.where(plane >= vals[k - 1], 1.0, 0.0)
    tied = jnp.where(reach == float(k), tied, 1.0)
    return vals, ids, tied


def _top_sums_sorted(cands, cand_ids, k):
    half = k // 2
    as_run = lambda lo, hi: (list(cands[lo:hi]), [jnp.full(cands[0].shape, i, F32)
                                                  for i in cand_ids[lo:hi]])
    runs = [as_run(0, k)]
    shorts = [as_run(lo, lo + half) for lo in range(k, len(cands), half)]
    assert len(shorts) % 2 == 0
    for (va, ia), (vb, ib) in zip(shorts[0::2], shorts[1::2]):
        vals, ids = va + vb[::-1], ia + ib[::-1]
        _bitonic_merge(vals, ids)
        runs.append((vals, ids))
    return _merge_runs(runs, cands, k)


def _lookup(table, sel):
    out = table[0]
    for a, plane in enumerate(table[1:], start=1):
        out = jnp.where(sel == float(a), plane, out)
    return out


def _score_planes(scores, scratch_ref):
    for g in range(SUBLANES):
        scratch_ref[g * SCORE_PITCH:g * SCORE_PITCH + PEER_N_KEYS, :] = (
            scores[:, g * LANES:(g + 1) * LANES])
    return [scratch_ref[pl.ds(n, SUBLANES, stride=SCORE_PITCH), :] for n in range(PEER_N_KEYS)]


def _peer_topk_kernel(h_ref, g_ref, wqt_ref, sk_ref, xn_ref, picks_ref, s0_ref, s1_ref):
    k = PEER_TOPK
    half = k // 2
    xn_ref[...] = (_rms_scale(h_ref[...]) * g_ref[...]).astype(BF16)
    key_ids = [float(n) for n in range(PEER_N_KEYS)]
    cand_ab = ([(0, b) for b in range(k)]
               + [(a, b) for a in range(1, half) for b in range(half)]
               + [(a, 0) for a in range(half, k)])
    cand_ids = [float(a * k + b) for a, b in cand_ab]

    def head_body(hd, carry):
        q0 = pl.multiple_of(hd * 2 * PEER_KEY_DIM, 2 * PEER_KEY_DIM)
        qt = lax.dot_general(wqt_ref[pl.ds(q0, 2 * PEER_KEY_DIM), :], xn_ref[...], _NT,
                             preferred_element_type=F32).astype(BF16)
        planes = []
        for p, scratch_ref in enumerate((s0_ref, s1_ref)):
            scores = jnp.dot(sk_ref[hd, p], qt[p * PEER_KEY_DIM:(p + 1) * PEER_KEY_DIM, :],
                             preferred_element_type=F32)
            planes.append(_score_planes(scores, scratch_ref))
        v1, i1, tied1 = _top_planes_sorted(planes[0], key_ids, k)
        v2, i2, tied2 = _top_planes_sorted(planes[1], key_ids, k)

        def exact_scan():
            return _top_planes(planes[0], key_ids, k) + _top_planes(planes[1], key_ids, k)

        v1, i1, v2, i2 = lax.cond(jnp.max(jnp.maximum(tied1, tied2)) > 0.0,
                                  exact_scan, lambda: (v1, i1, v2, i2))
        cands = [v1[a] + v2[b] for a, b in cand_ab]
        top_s, top_c, tied = _top_sums_sorted(cands, cand_ids, k)
        top_s, top_c = lax.cond(jnp.max(tied) > 0.0,
                                lambda: _top_planes(cands, cand_ids, k),
                                lambda: (top_s, top_c))
        denom = None
        gates = []
        for j in range(k):
            gates.append(jnp.exp(top_s[j] - top_s[0]))
            denom = gates[j] if denom is None else denom + gates[j]
        for j in range(k):
            a_sel = jnp.floor(top_c[j] * (1.0 / k))
            b_sel = top_c[j] - a_sel * k
            row0 = pl.multiple_of((hd * k + j) * SUBLANES, SUBLANES)
            picks_ref[0, 0, pl.ds(row0, SUBLANES), :] = _lookup(i1, a_sel)
            picks_ref[0, 1, pl.ds(row0, SUBLANES), :] = _lookup(i2, b_sel)
            picks_ref[0, 2, pl.ds(row0, SUBLANES), :] = gates[j] / denom
        return carry

    lax.fori_loop(0, PEER_HEADS, head_body, 0)


def _peer_topk(h, gain, w_query, subkeys):
    n, d = h.shape
    t = TOPK_TOKENS
    qdim = PEER_HEADS * 2 * PEER_KEY_DIM
    n_picks = PEER_HEADS * PEER_TOPK
    assert subkeys.shape == (PEER_HEADS, 2, PEER_N_KEYS, PEER_KEY_DIM)
    assert w_query.shape == (d, qdim) and n % t == 0
    return pl.pallas_call(
        _peer_topk_kernel,
        grid=(n // t,),
        in_specs=[
            pl.BlockSpec((t, d), lambda i: (i, 0)),
            pl.BlockSpec((1, d), lambda i: (0, 0)),
            pl.BlockSpec((qdim, d), lambda i: (0, 0)),
            pl.BlockSpec((PEER_HEADS, 2, PEER_N_KEYS, PEER_KEY_DIM), lambda i: (0, 0, 0, 0)),
        ],
        out_specs=[
            pl.BlockSpec((t, d), lambda i: (i, 0)),
            pl.BlockSpec((1, 3, n_picks * SUBLANES, LANES), lambda i: (i, 0, 0, 0)),
        ],
        out_shape=[
            jax.ShapeDtypeStruct((n, d), BF16),
            jax.ShapeDtypeStruct((n // t, 3, n_picks * SUBLANES, LANES), F32),
        ],
        scratch_shapes=[pltpu.VMEM((SUBLANES * SCORE_PITCH, LANES), F32)] * 2,
        compiler_params=_params("arbitrary"),
        name="peer_topk",
    )(h, gain.reshape(1, d), w_query.T.astype(BF16), subkeys.astype(BF16))


def _unpack_bf16_pair(words, which):
    if which == 0:
        return pltpu.bitcast(lax.shift_left(words, jnp.uint32(16)), F32)
    return pltpu.bitcast(words & jnp.uint32(0xFFFF0000), F32)


def _gate_build_kernel(picks_ref, gate_ref, i1t_s, i2t_s, gtt_s):
    n_picks = picks_ref.shape[2] // SUBLANES
    group = pl.program_id(1)
    for which, dst in enumerate((i1t_s, i2t_s, gtt_s)):
        dst[...] = picks_ref[0, which, pl.ds(group, n_picks, stride=SUBLANES), :].T
    sub_id = lax.broadcasted_iota(jnp.int32, (PEER_N_KEYS, n_picks), 0).astype(F32)

    def token_body(c, carry):
        i1r = i1t_s[pl.ds(c, 1), :]
        i2r = i2t_s[pl.ds(c, 1), :]
        gr = gtt_s[pl.ds(c, 1), :]
        a = jnp.where(sub_id == i1r, gr, 0.0).astype(BF16)
        b = jnp.where(sub_id == i2r, 1.0, 0.0).astype(BF16)
        g = lax.dot_general(a, b, _NT, preferred_element_type=F32).astype(BF16)
        for blk in range(N_EXPERT_BLOCKS):
            gate_ref[blk, c] = pltpu.bitcast(g[blk * EXPERT_ROWS:(blk + 1) * EXPERT_ROWS, :],
                                             jnp.uint32)
        return carry

    lax.fori_loop(0, GATE_TOKENS, token_body, 0, unroll=GATE_UNROLL)


def _gate_build(picks):
    tiles, _, pick_rows, _ = picks.shape
    n = tiles * TOPK_TOKENS
    n_picks = pick_rows // SUBLANES
    return pl.pallas_call(
        _gate_build_kernel,
        grid=(tiles, SUBLANES),
        in_specs=[pl.BlockSpec((1, 3, pick_rows, LANES), lambda i, g: (i, 0, 0, 0))],
        out_specs=pl.BlockSpec((N_EXPERT_BLOCKS, GATE_TOKENS, SUBLANES, PEER_N_KEYS),
                               lambda i, g: (0, i * SUBLANES + g, 0, 0)),
        out_shape=jax.ShapeDtypeStruct((N_EXPERT_BLOCKS, n, SUBLANES, PEER_N_KEYS), jnp.uint32),
        scratch_shapes=[pltpu.VMEM((GATE_TOKENS, n_picks), F32)] * 3,
        compiler_params=_params("arbitrary", "arbitrary"),
        name="peer_gate_build",
    )(picks)


def _peer_expert_kernel(xn_ref, gate_ref, u_ref, v_ref, h_ref, fg_ref, o_ref, coef_ref,
                        *, final_norm):
    e = pl.program_id(1)
    t = xn_ref.shape[0]

    @pl.when(e == 0)
    def _():
        o_ref[...] = h_ref[...]

    xn = xn_ref[...]
    per_chunk = EXPERT_CHUNK // PEER_N_KEYS
    for q in range(EXPERT_BLOCK // EXPERT_CHUNK):
        cols = slice(q * EXPERT_CHUNK, (q + 1) * EXPERT_CHUNK)
        act = lax.dot_general(xn, u_ref[cols, :], _NT, preferred_element_type=F32)
        gelu = 0.5 * act * (1.0 + lax.erf(act * (1.0 / math.sqrt(2.0))))
        assert per_chunk == 2
        words = gate_ref[0, pl.ds(q, t, stride=SUBLANES), :]
        gate = jnp.concatenate([_unpack_bf16_pair(words, 0), _unpack_bf16_pair(words, 1)], axis=1)
        coef_ref[:, cols] = (gate * gelu).astype(BF16)
    o_ref[...] += jnp.dot(coef_ref[...], v_ref[...], preferred_element_type=F32)

    if final_norm:
        @pl.when(e == pl.num_programs(1) - 1)
        def _():
            o_ref[...] = _rms_scale(o_ref[...]) * fg_ref[...]


def _peer_experts(xn, gate, u_tbl, v_tbl, h, final_gain, final_norm):
    n, d = h.shape
    t = min(EXPERT_TOKENS, n)
    assert n % t == 0 and u_tbl.shape == (N_EXPERT_BLOCKS * EXPERT_BLOCK, d)
    gate2 = gate.reshape(gate.shape[0], n * SUBLANES, PEER_N_KEYS)
    return pl.pallas_call(
        functools.partial(_peer_expert_kernel, final_norm=final_norm),
        grid=(n // t, N_EXPERT_BLOCKS),
        in_specs=[
            pl.BlockSpec((t, d), lambda i, e: (i, 0)),
            pl.BlockSpec((1, t * SUBLANES, PEER_N_KEYS), lambda i, e: (e, i, 0)),
            pl.BlockSpec((EXPERT_BLOCK, d), lambda i, e: (e, 0)),
            pl.BlockSpec((EXPERT_BLOCK, d), lambda i, e: (e, 0)),
            pl.BlockSpec((t, d), lambda i, e: (i, 0)),
            pl.BlockSpec((1, d), lambda i, e: (0, 0)),
        ],
        out_specs=pl.BlockSpec((t, d), lambda i, e: (i, 0)),
        out_shape=jax.ShapeDtypeStruct((n, d), F32),
        scratch_shapes=[pltpu.VMEM((t, EXPERT_BLOCK), BF16)],
        compiler_params=_params("arbitrary", "arbitrary"),
        name="peer_experts",
    )(xn, gate2, u_tbl.astype(BF16), v_tbl.astype(BF16), h, final_gain.reshape(1, d))


def _peer_ffn(h, gain, w_query, subkeys, u_tbl, v_tbl, final_gain, final_norm):
    xn, picks = _peer_topk(h, gain, w_query, subkeys)
    return _peer_experts(xn, _gate_build(picks), u_tbl, v_tbl, h, final_gain, final_norm)


def _kvq_kernel(h_ref, hp_ref, gkv_ref, gq_ref, wk_ref, wvt_ref, wqt_ref,
                k_ref, vt_ref, qt_ref):
    n_heads = qt_ref.shape[1]
    t = h_ref.shape[0]
    xkv = (_rms_scale(hp_ref[...]) * gkv_ref[...]).astype(BF16)
    xq = (_rms_scale(h_ref[...]) * gq_ref[...]).astype(BF16)
    k_ref[...] = jnp.dot(xkv, wk_ref[...], preferred_element_type=F32).astype(BF16)
    vt_ref[0] = lax.dot_general(wvt_ref[...], xkv, _NT,
                                preferred_element_type=F32).astype(BF16)
    qt = (lax.dot_general(wqt_ref[...], xq, _NT, preferred_element_type=F32)
          * (0.5 * HEAD_DIM ** -0.5)).astype(BF16)
    zeros = jnp.zeros((HEAD_DIM, t), BF16)
    for hd in range(n_heads):
        q_h = qt[hd * HEAD_DIM:(hd + 1) * HEAD_DIM, :]
        if hd % 2 == 0:
            qt_ref[0, hd, 0:HEAD_DIM, :] = q_h
            qt_ref[0, hd, HEAD_DIM:2 * HEAD_DIM, :] = zeros
        else:
            qt_ref[0, hd, 0:HEAD_DIM, :] = zeros
            qt_ref[0, hd, HEAD_DIM:2 * HEAD_DIM, :] = q_h


def _key_permute(x, seq_len):
    n, d = x.shape
    kb = min(ATTN_BLOCK, seq_len)
    seg = kb // SUBLANES
    return x.reshape(n // kb, SUBLANES, seg, d).swapaxes(1, 2).reshape(n, d)


def _kvq_proj(h, g_kv, g_q, w_kv, w_q, batch, seq_len):
    n, d = h.shape
    n_heads = d // HEAD_DIM
    t = min(PROJ_TOKENS, seq_len)
    tiles = seq_len // t
    assert seq_len % t == 0 and t % min(ATTN_BLOCK, seq_len) == 0 and n_heads % 2 == 0
    w_k, w_v = w_kv[:, :d], w_kv[:, d:]
    return pl.pallas_call(
        _kvq_kernel,
        grid=(n // t,),
        in_specs=[
            pl.BlockSpec((t, d), lambda i: (i, 0)),
            pl.BlockSpec((t, d), lambda i: (i, 0)),
            pl.BlockSpec((1, d), lambda i: (0, 0)),
            pl.BlockSpec((1, d), lambda i: (0, 0)),
            pl.BlockSpec((d, d), lambda i: (0, 0)),
            pl.BlockSpec((d, d), lambda i: (0, 0)),
            pl.BlockSpec((d, d), lambda i: (0, 0)),
        ],
        out_specs=[
            pl.BlockSpec((t, d), lambda i: (i, 0)),
            pl.BlockSpec((1, d, t), lambda i: (i // tiles, 0, i % tiles)),
            pl.BlockSpec((1, n_heads, 2 * HEAD_DIM, t), lambda i: (i // tiles, 0, 0, i % tiles)),
        ],
        out_shape=[
            jax.ShapeDtypeStruct((n, d), BF16),
            jax.ShapeDtypeStruct((batch, d, seq_len), BF16),
            jax.ShapeDtypeStruct((batch, n_heads, 2 * HEAD_DIM, seq_len), BF16),
        ],
        compiler_params=_params("arbitrary"),
        name="kvq_proj",
    )(h, _key_permute(h, seq_len), g_kv.reshape(1, d), g_q.reshape(1, d),
      w_k.astype(BF16), w_v.T.astype(BF16), w_q.T.astype(BF16))


def _attn_kernel(qt_ref, k_ref, vt_ref, bias_ref, o_ref, *, blk, tiles):
    seg = blk // SUBLANES
    width = 2 * blk
    first_tile = pl.program_id(2) * tiles
    qts = [jnp.concatenate([qt_ref[0, 0, :, u * blk:(u + 1) * blk],
                            qt_ref[0, 1, :, u * blk:(u + 1) * blk]], axis=1)
           for u in range(tiles)]

    def block(u, kb, carry, acc0, acc1, masked):
        k0 = pl.multiple_of(kb * blk, blk)
        zh = jnp.dot(k_ref[0, pl.ds(k0, blk), :], qts[u], preferred_element_type=F32)
        if masked:
            zh = zh + bias_ref[...]
        ht = 0.5 * jnp.tanh(zh)
        beta = 0.5 + ht
        keep = 0.5 - ht
        run = jnp.ones((SUBLANES, width), F32)
        parts = [None] * seg
        for rr in reversed(range(seg)):
            parts[rr] = run
            run = run * keep[rr * SUBLANES:(rr + 1) * SUBLANES, :]
        offs = [None] * SUBLANES
        c = carry
        for s in reversed(range(SUBLANES)):
            offs[s] = c
            c = c * run[s:s + 1, :]
        off = jnp.concatenate(offs, axis=0)
        w = jnp.concatenate(
            [beta[rr * SUBLANES:(rr + 1) * SUBLANES, :] * (parts[rr] * off)
             for rr in range(seg)], axis=0).astype(BF16)
        vt = vt_ref[0, :, pl.ds(k0, blk)]
        acc0 = acc0 + jnp.dot(vt[0:HEAD_DIM, :], w[:, 0:blk], preferred_element_type=F32)
        acc1 = acc1 + jnp.dot(vt[HEAD_DIM:, :], w[:, blk:], preferred_element_type=F32)
        return c, acc0, acc1

    ones = jnp.ones((1, width), F32)
    zero = jnp.zeros((HEAD_DIM, blk), F32)
    state = (jnp.int32(0),)
    for u in range(tiles):
        state = state + block(u, first_tile + u, ones, zero, zero, True)

    def live_carries(st):
        return [jnp.where(st[0] < first_tile + u, st[1 + 3 * u], 0.0) for u in range(tiles)]

    def cond(st):
        return jnp.max(functools.reduce(jnp.maximum, live_carries(st))) > 0.0

    def body(st):
        new = (st[0] + 1,)
        for u, carry in enumerate(live_carries(st)):
            kb = jnp.maximum(first_tile + u - 1 - st[0], 0)
            new = new + block(u, kb, carry, st[2 + 3 * u], st[3 + 3 * u], False)
        return new

    state = lax.while_loop(cond, body, state)
    for u in range(tiles):
        o_ref[0, 0:HEAD_DIM, u * blk:(u + 1) * blk] = state[2 + 3 * u]
        o_ref[0, HEAD_DIM:, u * blk:(u + 1) * blk] = state[3 + 3 * u]


def _causal_bias(blk):
    seg = blk // SUBLANES
    row = jnp.arange(blk, dtype=jnp.int32)[:, None]
    key_local = (row % SUBLANES) * seg + row // SUBLANES
    query_local = jnp.arange(2 * blk, dtype=jnp.int32)[None, :] % blk
    return jnp.where(key_local < query_local, 0.0, -1e30).astype(F32)


def _stick_breaking_attention(qt_pad, k_perm, vt_perm, seq_len):
    batch, n_heads = qt_pad.shape[0], qt_pad.shape[1]
    d = n_heads * HEAD_DIM
    blk = min(ATTN_BLOCK, seq_len)
    tiles = min(ATTN_TILES, seq_len // blk)
    assert seq_len % (blk * tiles) == 0
    k3 = k_perm.reshape(batch, seq_len, d)
    return pl.pallas_call(
        functools.partial(_attn_kernel, blk=blk, tiles=tiles),
        grid=(batch, n_heads // 2, seq_len // (blk * tiles)),
        in_specs=[
            pl.BlockSpec((1, 2, 2 * HEAD_DIM, blk * tiles), lambda b, p, q: (b, p, 0, q)),
            pl.BlockSpec((1, seq_len, 2 * HEAD_DIM), lambda b, p, q: (b, 0, p)),
            pl.BlockSpec((1, 2 * HEAD_DIM, seq_len), lambda b, p, q: (b, p, 0)),
            pl.BlockSpec((blk, 2 * blk), lambda b, p, q: (0, 0)),
        ],
        out_specs=pl.BlockSpec((1, 2 * HEAD_DIM, blk * tiles), lambda b, p, q: (b, p, q)),
        out_shape=jax.ShapeDtypeStruct((batch, d, seq_len), F32),
        compiler_params=_params("arbitrary", "arbitrary", "arbitrary"),
        name="stick_breaking_attention",
    )(qt_pad, k3, vt_perm, _causal_bias(blk))


def _out_proj_kernel(ot_ref, w_ref, h_ref, o_ref):
    a = ot_ref[0].T.astype(BF16)
    o_ref[...] = h_ref[...] + jnp.dot(a, w_ref[...], preferred_element_type=F32)


def _out_proj(ot, w_o, h, seq_len):
    n, d = h.shape
    t = min(PROJ_TOKENS, seq_len)
    tiles = seq_len // t
    return pl.pallas_call(
        _out_proj_kernel,
        grid=(n // t,),
        in_specs=[
            pl.BlockSpec((1, d, t), lambda i: (i // tiles, 0, i % tiles)),
            pl.BlockSpec((d, d), lambda i: (0, 0)),
            pl.BlockSpec((t, d), lambda i: (i, 0)),
        ],
        out_specs=pl.BlockSpec((t, d), lambda i: (i, 0)),
        out_shape=jax.ShapeDtypeStruct((n, d), F32),
        compiler_params=_params("arbitrary"),
        name="attn_out_proj",
    )(ot, w_o.astype(BF16), h)


def kernel(x, norm_mix, norm_ffn, conv_w_in, conv_kernel, conv_w_out, norm_kv, w_kv,
           attn_w_q, attn_w_o, peer_w_query, peer_subkeys, peer_u, peer_v, norm_final):
    batch, seq_len, d = x.shape
    assert norm_mix.shape[0] == 2 and conv_w_in.shape[0] == 1 and attn_w_q.shape[0] == 1
    h = x.reshape(batch * seq_len, d)
    h = _conv_mixer(h, norm_mix[0], conv_w_in[0], conv_kernel[0], conv_w_out[0], seq_len)
    h = _peer_ffn(h, norm_ffn[0], peer_w_query[0], peer_subkeys[0], peer_u[0], peer_v[0],
                  norm_final, final_norm=False)
    k_perm, vt_perm, qt_pad = _kvq_proj(h, norm_kv, norm_mix[1], w_kv, attn_w_q[0],
                                        batch, seq_len)
    ot = _stick_breaking_attention(qt_pad, k_perm, vt_perm, seq_len)
    h = _out_proj(ot, attn_w_o[0], h, seq_len)
    h = _peer_ffn(h, norm_ffn[1], peer_w_query[1], peer_subkeys[1], peer_u[1], peer_v[1],
                  norm_final, final_norm=True)
    return h.reshape(batch, seq_len, d)
```

```python
import functools
import math

import jax
import jax.numpy as jnp
from jax import lax
from jax.experimental import pallas as pl
from jax.experimental.pallas import tpu as pltpu

F32 = jnp.float32
BF16 = jnp.bfloat16

RMS_EPS = 1e-6
CONV_WIDTH = 3
HEAD_DIM = 64
PEER_HEADS = 8
PEER_N_KEYS = 128
PEER_KEY_DIM = 128
PEER_TOPK = 16

SUBLANES = 8
LANES = 128
VMEM_LIMIT_BYTES = 56 * 1024 * 1024

CONV_TOKENS = 512
TOPK_TOKENS = SUBLANES * LANES
SCORE_PITCH = PEER_N_KEYS + SUBLANES
TOPK_CHAINS = 4
GATE_TOKENS = LANES
GATE_UNROLL = 128
EXPERT_TOKENS = 1024
EXPERT_ROWS = 2 * SUBLANES
EXPERT_BLOCK = EXPERT_ROWS * PEER_N_KEYS
N_EXPERT_BLOCKS = PEER_N_KEYS // EXPERT_ROWS
EXPERT_CHUNK = 2 * PEER_N_KEYS
PROJ_TOKENS = 512
ATTN_BLOCK = 128
ATTN_TILES = 16

_NT = (((1,), (1,)), ((), ()))


def _params(*sem):
    return pltpu.CompilerParams(dimension_semantics=sem,
                                vmem_limit_bytes=VMEM_LIMIT_BYTES)


def _rms_scale(x):
    return x * lax.rsqrt(jnp.mean(x * x, axis=-1, keepdims=True) + RMS_EPS)


def _conv_mixer_kernel(x_ref, g_ref, win_ref, ck_ref, wout_ref, o_ref, ubuf_ref,
                       *, tiles_per_seq):
    t = x_ref.shape[0]
    d = x_ref.shape[1]
    halo = SUBLANES
    first = (pl.program_id(0) % tiles_per_seq) == 0

    @pl.when(first)
    def _():
        ubuf_ref[0:halo, :] = jnp.zeros((halo, d), F32)

    @pl.when(jnp.logical_not(first))
    def _():
        ubuf_ref[0:halo, :] = ubuf_ref[t:t + halo, :]

    x = x_ref[...]
    xb = (_rms_scale(x) * g_ref[...]).astype(BF16)
    bg = jnp.dot(xb, win_ref[:, 0:d], preferred_element_type=F32)
    cg = jnp.dot(xb, win_ref[:, d:2 * d], preferred_element_type=F32)
    hv = jnp.dot(xb, win_ref[:, 2 * d:3 * d], preferred_element_type=F32)
    u = cg * hv
    ubuf_ref[halo:halo + t, :] = u
    conv = ubuf_ref[halo - 2:halo - 2 + t, :] * ck_ref[0:1, :]
    conv = conv + ubuf_ref[halo - 1:halo - 1 + t, :] * ck_ref[1:2, :]
    conv = conv + u * ck_ref[2:3, :]
    y = (bg * conv).astype(BF16)
    o_ref[...] = x + jnp.dot(y, wout_ref[...], preferred_element_type=F32)


def _conv_mixer(h, gain, w_in, conv_k, w_out, seq_len):
    n, d = h.shape
    t = min(CONV_TOKENS, seq_len)
    assert seq_len % t == 0 and conv_k.shape[0] == CONV_WIDTH
    return pl.pallas_call(
        functools.partial(_conv_mixer_kernel, tiles_per_seq=seq_len // t),
        grid=(n // t,),
        in_specs=[
            pl.BlockSpec((t, d), lambda i: (i, 0)),
            pl.BlockSpec((1, d), lambda i: (0, 0)),
            pl.BlockSpec((d, 3 * d), lambda i: (0, 0)),
            pl.BlockSpec((CONV_WIDTH, d), lambda i: (0, 0)),
            pl.BlockSpec((d, d), lambda i: (0, 0)),
        ],
        out_specs=pl.BlockSpec((t, d), lambda i: (i, 0)),
        out_shape=jax.ShapeDtypeStruct((n, d), F32),
        scratch_shapes=[pltpu.VMEM((t + 2 * SUBLANES, d), F32)],
        compiler_params=_params("arbitrary"),
        name="conv_mixer",
    )(h, gain.reshape(1, d), w_in.astype(BF16), conv_k, w_out.astype(BF16))


def _top_planes(planes, ids, k):
    planes = list(planes)
    chunk = -(-len(planes) // TOPK_CHAINS)
    vals, picked = [], []
    for it in range(k):
        best = best_id = None
        for lo in range(0, len(planes), chunk):
            part = planes[lo]
            part_id = jnp.full(part.shape, ids[lo], F32)
            for plane, plane_id in zip(planes[lo + 1:lo + chunk], ids[lo + 1:lo + chunk]):
                better = plane > part
                part = jnp.where(better, plane, part)
                part_id = jnp.where(better, plane_id, part_id)
            if best is None:
                best, best_id = part, part_id
            else:
                better = part > best
                best = jnp.where(better, part, best)
                best_id = jnp.where(better, part_id, best_id)
        vals.append(best)
        picked.append(best_id)
        if it + 1 < k:
            planes = [jnp.where(best_id == plane_id, -jnp.inf, plane)
                      for plane, plane_id in zip(planes, ids)]
    return vals, picked


def _sort_network(n):
    pairs = []

    def merge(lo, count, stride):
        step = stride * 2
        if step < count:
            merge(lo, count, step)
            merge(lo + stride, count, step)
            pairs.extend((i, i + stride) for i in range(lo + stride, lo + count - stride, step))
        else:
            pairs.append((lo, lo + stride))

    def sort(lo, count):
        if count > 1:
            sort(lo, count // 2)
            sort(lo + count // 2, count // 2)
            merge(lo, count, 1)

    sort(0, n)
    return pairs


def _order(vals, ids, i, j):
    swap = vals[j] > vals[i]
    vals[i], vals[j] = jnp.where(swap, vals[j], vals[i]), jnp.where(swap, vals[i], vals[j])
    ids[i], ids[j] = jnp.where(swap, ids[j], ids[i]), jnp.where(swap, ids[i], ids[j])


def _top_planes_sorted(planes, ids, k):
    assert len(planes) % k == 0 and k & (k - 1) == 0
    network = _sort_network(k)
    groups = []
    for lo in range(0, len(planes), k):
        vals = list(planes[lo:lo + k])
        gids = [jnp.full(vals[0].shape, i, F32) for i in ids[lo:lo + k]]
        for i, j in network:
            _order(vals, gids, i, j)
        groups.append((vals, gids))
    return _merge_runs(groups, planes, k)


def _bitonic_merge(vals, ids):
    dist = len(vals) // 2
    while dist:
        for i in range(len(vals)):
            if i & dist == 0:
                _order(vals, ids, i, i + dist)
        dist //= 2


def _merge_runs(runs, planes, k):
    while len(runs) > 1:
        merged = [runs[-1]] if len(runs) % 2 else []
        for (va, ia), (vb, ib) in zip(runs[0:-1:2], runs[1::2]):
            take_b = [vb[k - 1 - i] > va[i] for i in range(k)]
            vals = [jnp.where(t, vb[k - 1 - i], va[i]) for i, t in enumerate(take_b)]
            ids = [jnp.where(t, ib[k - 1 - i], ia[i]) for i, t in enumerate(take_b)]
            _bitonic_merge(vals, ids)
            merged.append((vals, ids))
        runs = merged
    vals, ids = runs[0]
    tied = jnp.zeros(vals[0].shape, F32)
    for i in range(k - 1):
        tied = jnp.where(vals[i] > vals[i + 1], tied, 1.0)
    reach = jnp.zeros(vals[0].shape, F32)
    for plane in planes:
        reach = reach + jnp.where(plane >= vals[k - 1], 1.0, 0.0)
    tied = jnp.where(reach == float(k), tied, 1.0)
    return vals, ids, tied


def _top_sums_sorted(cands, cand_ids, k):
    half = k // 2
    as_run = lambda lo, hi: (list(cands[lo:hi]), [jnp.full(cands[0].shape, i, F32)
                                                  for i in cand_ids[lo:hi]])
    runs = [as_run(0, k)]
    shorts = [as_run(lo, lo + half) for lo in range(k, len(cands), half)]
    assert len(shorts) % 2 == 0
    for (va, ia), (vb, ib) in zip(shorts[0::2], shorts[1::2]):
        vals, ids = va + vb[::-1], ia + ib[::-1]
        _bitonic_merge(vals, ids)
        runs.append((vals, ids))
    return _merge_runs(runs, cands, k)


def _lookup(table, sel):
    out = table[0]
    for a, plane in enumerate(table[1:], start=1):
        out = jnp.where(sel == float(a), plane, out)
    return out


def _score_planes(scores, scratch_ref):
    for g in range(SUBLANES):
        scratch_ref[g * SCORE_PITCH:g * SCORE_PITCH + PEER_N_KEYS, :] = (
            scores[:, g * LANES:(g + 1) * LANES])
    return [scratch_ref[pl.ds(n, SUBLANES, stride=SCORE_PITCH), :] for n in range(PEER_N_KEYS)]


def _peer_topk_kernel(h_ref, g_ref, wqt_ref, sk_ref, xn_ref, picks_ref, s0_ref, s1_ref):
    k = PEER_TOPK
    half = k // 2
    xn_ref[...] = (_rms_scale(h_ref[...]) * g_ref[...]).astype(BF16)
    key_ids = [float(n) for n in range(PEER_N_KEYS)]
    cand_ab = ([(0, b) for b in range(k)]
               + [(a, b) for a in range(1, half) for b in range(half)]
               + [(a, 0) for a in range(half, k)])
    cand_ids = [float(a * k + b) for a, b in cand_ab]

    def head_body(hd, carry):
        q0 = pl.multiple_of(hd * 2 * PEER_KEY_DIM, 2 * PEER_KEY_DIM)
        qt = lax.dot_general(wqt_ref[pl.ds(q0, 2 * PEER_KEY_DIM), :], xn_ref[...], _NT,
                             preferred_element_type=F32).astype(BF16)
        planes = []
        for p, scratch_ref in enumerate((s0_ref, s1_ref)):
            scores = jnp.dot(sk_ref[hd, p], qt[p * PEER_KEY_DIM:(p + 1) * PEER_KEY_DIM, :],
                             preferred_element_type=F32)
            planes.append(_score_planes(scores, scratch_ref))
        v1, i1, tied1 = _top_planes_sorted(planes[0], key_ids, k)
        v2, i2, tied2 = _top_planes_sorted(planes[1], key_ids, k)

        def exact_scan():
            return _top_planes(planes[0], key_ids, k) + _top_planes(planes[1], key_ids, k)

        v1, i1, v2, i2 = lax.cond(jnp.max(jnp.maximum(tied1, tied2)) > 0.0,
                                  exact_scan, lambda: (v1, i1, v2, i2))
        cands = [v1[a] + v2[b] for a, b in cand_ab]
        top_s, top_c, tied = _top_sums_sorted(cands, cand_ids, k)
        top_s, top_c = lax.cond(jnp.max(tied) > 0.0,
                                lambda: _top_planes(cands, cand_ids, k),
                                lambda: (top_s, top_c))
        denom = None
        gates = []
        for j in range(k):
            gates.append(jnp.exp(top_s[j] - top_s[0]))
            denom = gates[j] if denom is None else denom + gates[j]
        for j in range(k):
            a_sel = jnp.floor(top_c[j] * (1.0 / k))
            b_sel = top_c[j] - a_sel * k
            row0 = pl.multiple_of((hd * k + j) * SUBLANES, SUBLANES)
            picks_ref[0, 0, pl.ds(row0, SUBLANES), :] = _lookup(i1, a_sel)
            picks_ref[0, 1, pl.ds(row0, SUBLANES), :] = _lookup(i2, b_sel)
            picks_ref[0, 2, pl.ds(row0, SUBLANES), :] = gates[j] / denom
        return carry

    lax.fori_loop(0, PEER_HEADS, head_body, 0)


def _peer_topk(h, gain, w_query, subkeys):
    n, d = h.shape
    t = TOPK_TOKENS
    qdim = PEER_HEADS * 2 * PEER_KEY_DIM
    n_picks = PEER_HEADS * PEER_TOPK
    assert subkeys.shape == (PEER_HEADS, 2, PEER_N_KEYS, PEER_KEY_DIM)
    assert w_query.shape == (d, qdim) and n % t == 0
    return pl.pallas_call(
        _peer_topk_kernel,
        grid=(n // t,),
        in_specs=[
            pl.BlockSpec((t, d), lambda i: (i, 0)),
            pl.BlockSpec((1, d), lambda i: (0, 0)),
            pl.BlockSpec((qdim, d), lambda i: (0, 0)),
            pl.BlockSpec((PEER_HEADS, 2, PEER_N_KEYS, PEER_KEY_DIM), lambda i: (0, 0, 0, 0)),
        ],
        out_specs=[
            pl.BlockSpec((t, d), lambda i: (i, 0)),
            pl.BlockSpec((1, 3, n_picks * SUBLANES, LANES), lambda i: (i, 0, 0, 0)),
        ],
        out_shape=[
            jax.ShapeDtypeStruct((n, d), BF16),
            jax.ShapeDtypeStruct((n // t, 3, n_picks * SUBLANES, LANES), F32),
        ],
        scratch_shapes=[pltpu.VMEM((SUBLANES * SCORE_PITCH, LANES), F32)] * 2,
        compiler_params=_params("arbitrary"),
        name="peer_topk",
    )(h, gain.reshape(1, d), w_query.T.astype(BF16), subkeys.astype(BF16))


def _unpack_bf16_pair(words, which):
    if which == 0:
        return pltpu.bitcast(lax.shift_left(words, jnp.uint32(16)), F32)
    return pltpu.bitcast(words & jnp.uint32(0xFFFF0000), F32)


def _gate_build_kernel(picks_ref, gate_ref, i1t_s, i2t_s, gtt_s):
    n_picks = picks_ref.shape[2] // SUBLANES
    group = pl.program_id(1)
    for which, dst in enumerate((i1t_s, i2t_s, gtt_s)):
        dst[...] = picks_ref[0, which, pl.ds(group, n_picks, stride=SUBLANES), :].T
    sub_id = lax.broadcasted_iota(jnp.int32, (PEER_N_KEYS, n_picks), 0).astype(F32)

    def token_body(c, carry):
        i1r = i1t_s[pl.ds(c, 1), :]
        i2r = i2t_s[pl.ds(c, 1), :]
        gr = gtt_s[pl.ds(c, 1), :]
        a = jnp.where(sub_id == i1r, gr, 0.0).astype(BF16)
        b = jnp.where(sub_id == i2r, 1.0, 0.0).astype(BF16)
        g = lax.dot_general(a, b, _NT, preferred_element_type=F32).astype(BF16)
        for blk in range(N_EXPERT_BLOCKS):
            gate_ref[blk, c] = pltpu.bitcast(g[blk * EXPERT_ROWS:(blk + 1) * EXPERT_ROWS, :],
                                             jnp.uint32)
        return carry

    lax.fori_loop(0, GATE_TOKENS, token_body, 0, unroll=GATE_UNROLL)


def _gate_build(picks):
    tiles, _, pick_rows, _ = picks.shape
    n = tiles * TOPK_TOKENS
    n_picks = pick_rows // SUBLANES
    return pl.pallas_call(
        _gate_build_kernel,
        grid=(tiles, SUBLANES),
        in_specs=[pl.BlockSpec((1, 3, pick_rows, LANES), lambda i, g: (i, 0, 0, 0))],
        out_specs=pl.BlockSpec((N_EXPERT_BLOCKS, GATE_TOKENS, SUBLANES, PEER_N_KEYS),
                               lambda i, g: (0, i * SUBLANES + g, 0, 0)),
        out_shape=jax.ShapeDtypeStruct((N_EXPERT_BLOCKS, n, SUBLANES, PEER_N_KEYS), jnp.uint32),
        scratch_shapes=[pltpu.VMEM((GATE_TOKENS, n_picks), F32)] * 3,
        compiler_params=_params("arbitrary", "arbitrary"),
        name="peer_gate_build",
    )(picks)


def _peer_expert_kernel(xn_ref, gate_ref, u_ref, v_ref, h_ref, fg_ref, o_ref, coef_ref,
                        *, final_norm):
    e = pl.program_id(1)
    t = xn_ref.shape[0]

    @pl.when(e == 0)
    def _():
        o_ref[...] = h_ref[...]

    xn = xn_ref[...]
    per_chunk = EXPERT_CHUNK // PEER_N_KEYS
    for q in range(EXPERT_BLOCK // EXPERT_CHUNK):
        cols = slice(q * EXPERT_CHUNK, (q + 1) * EXPERT_CHUNK)
        act = lax.dot_general(xn, u_ref[cols, :], _NT, preferred_element_type=F32)
        gelu = 0.5 * act * (1.0 + lax.erf(act * (1.0 / math.sqrt(2.0))))
        assert per_chunk == 2
        words = gate_ref[0, pl.ds(q, t, stride=SUBLANES), :]
        gate = jnp.concatenate([_unpack_bf16_pair(words, 0), _unpack_bf16_pair(words, 1)], axis=1)
        coef_ref[:, cols] = (gate * gelu).astype(BF16)
    o_ref[...] += jnp.dot(coef_ref[...], v_ref[...], preferred_element_type=F32)

    if final_norm:
        @pl.when(e == pl.num_programs(1) - 1)
        def _():
            o_ref[...] = _rms_scale(o_ref[...]) * fg_ref[...]


def _peer_experts(xn, gate, u_tbl, v_tbl, h, final_gain, final_norm):
    n, d = h.shape
    t = min(EXPERT_TOKENS, n)
    assert n % t == 0 and u_tbl.shape == (N_EXPERT_BLOCKS * EXPERT_BLOCK, d)
    gate2 = gate.reshape(gate.shape[0], n * SUBLANES, PEER_N_KEYS)
    return pl.pallas_call(
        functools.partial(_peer_expert_kernel, final_norm=final_norm),
        grid=(n // t, N_EXPERT_BLOCKS),
        in_specs=[
            pl.BlockSpec((t, d), lambda i, e: (i, 0)),
            pl.BlockSpec((1, t * SUBLANES, PEER_N_KEYS), lambda i, e: (e, i, 0)),
            pl.BlockSpec((EXPERT_BLOCK, d), lambda i, e: (e, 0)),
            pl.BlockSpec((EXPERT_BLOCK, d), lambda i, e: (e, 0)),
            pl.BlockSpec((t, d), lambda i, e: (i, 0)),
            pl.BlockSpec((1, d), lambda i, e: (0, 0)),
        ],
        out_specs=pl.BlockSpec((t, d), lambda i, e: (i, 0)),
        out_shape=jax.ShapeDtypeStruct((n, d), F32),
        scratch_shapes=[pltpu.VMEM((t, EXPERT_BLOCK), BF16)],
        compiler_params=_params("arbitrary", "arbitrary"),
        name="peer_experts",
    )(xn, gate2, u_tbl.astype(BF16), v_tbl.astype(BF16), h, final_gain.reshape(1, d))


def _peer_ffn(h, gain, w_query, subkeys, u_tbl, v_tbl, final_gain, final_norm):
    xn, picks = _peer_topk(h, gain, w_query, subkeys)
    return _peer_experts(xn, _gate_build(picks), u_tbl, v_tbl, h, final_gain, final_norm)


def _kvq_kernel(h_ref, hp_ref, gkv_ref, gq_ref, wk_ref, wvt_ref, wqt_ref,
                k_ref, vt_ref, qt_ref):
    n_heads = qt_ref.shape[1]
    t = h_ref.shape[0]
    xkv = (_rms_scale(hp_ref[...]) * gkv_ref[...]).astype(BF16)
    xq = (_rms_scale(h_ref[...]) * gq_ref[...]).astype(BF16)
    k_ref[...] = jnp.dot(xkv, wk_ref[...], preferred_element_type=F32).astype(BF16)
    vt_ref[0] = lax.dot_general(wvt_ref[...], xkv, _NT,
                                preferred_element_type=F32).astype(BF16)
    qt = (lax.dot_general(wqt_ref[...], xq, _NT, preferred_element_type=F32)
          * (0.5 * HEAD_DIM ** -0.5)).astype(BF16)
    zeros = jnp.zeros((HEAD_DIM, t), BF16)
    for hd in range(n_heads):
        q_h = qt[hd * HEAD_DIM:(hd + 1) * HEAD_DIM, :]
        if hd % 2 == 0:
            qt_ref[0, hd, 0:HEAD_DIM, :] = q_h
            qt_ref[0, hd, HEAD_DIM:2 * HEAD_DIM, :] = zeros
        else:
            qt_ref[0, hd, 0:HEAD_DIM, :] = zeros
            qt_ref[0, hd, HEAD_DIM:2 * HEAD_DIM, :] = q_h


def _key_permute(x, seq_len):
    n, d = x.shape
    kb = min(ATTN_BLOCK, seq_len)
    seg = kb // SUBLANES
    return x.reshape(n // kb, SUBLANES, seg, d).swapaxes(1, 2).reshape(n, d)


def _kvq_proj(h, g_kv, g_q, w_kv, w_q, batch, seq_len):
    n, d = h.shape
    n_heads = d // HEAD_DIM
    t = min(PROJ_TOKENS, seq_len)
    tiles = seq_len // t
    assert seq_len % t == 0 and t % min(ATTN_BLOCK, seq_len) == 0 and n_heads % 2 == 0
    w_k, w_v = w_kv[:, :d], w_kv[:, d:]
    return pl.pallas_call(
        _kvq_kernel,
        grid=(n // t,),
        in_specs=[
            pl.BlockSpec((t, d), lambda i: (i, 0)),
            pl.BlockSpec((t, d), lambda i: (i, 0)),
            pl.BlockSpec((1, d), lambda i: (0, 0)),
            pl.BlockSpec((1, d), lambda i: (0, 0)),
            pl.BlockSpec((d, d), lambda i: (0, 0)),
            pl.BlockSpec((d, d), lambda i: (0, 0)),
            pl.BlockSpec((d, d), lambda i: (0, 0)),
        ],
        out_specs=[
            pl.BlockSpec((t, d), lambda i: (i, 0)),
            pl.BlockSpec((1, d, t), lambda i: (i // tiles, 0, i % tiles)),
            pl.BlockSpec((1, n_heads, 2 * HEAD_DIM, t), lambda i: (i // tiles, 0, 0, i % tiles)),
        ],
        out_shape=[
            jax.ShapeDtypeStruct((n, d), BF16),
            jax.ShapeDtypeStruct((batch, d, seq_len), BF16),
            jax.ShapeDtypeStruct((batch, n_heads, 2 * HEAD_DIM, seq_len), BF16),
        ],
        compiler_params=_params("arbitrary"),
        name="kvq_proj",
    )(h, _key_permute(h, seq_len), g_kv.reshape(1, d), g_q.reshape(1, d),
      w_k.astype(BF16), w_v.T.astype(BF16), w_q.T.astype(BF16))


def _attn_kernel(qt_ref, k_ref, vt_ref, bias_ref, o_ref, *, blk, tiles):
    seg = blk // SUBLANES
    width = 2 * blk
    first_tile = pl.program_id(2) * tiles
    qts = [jnp.concatenate([qt_ref[0, 0, :, u * blk:(u + 1) * blk],
                            qt_ref[0, 1, :, u * blk:(u + 1) * blk]], axis=1)
           for u in range(tiles)]

    def block(u, kb, carry, acc0, acc1, masked):
        k0 = pl.multiple_of(kb * blk, blk)
        zh = jnp.dot(k_ref[0, pl.ds(k0, blk), :], qts[u], preferred_element_type=F32)
        if masked:
            zh = zh + bias_ref[...]
        ht = 0.5 * jnp.tanh(zh)
        beta = 0.5 + ht
        keep = 0.5 - ht
        run = jnp.ones((SUBLANES, width), F32)
        parts = [None] * seg
        for rr in reversed(range(seg)):
            parts[rr] = run
            run = run * keep[rr * SUBLANES:(rr + 1) * SUBLANES, :]
        offs = [None] * SUBLANES
        c = carry
        for s in reversed(range(SUBLANES)):
            offs[s] = c
            c = c * run[s:s + 1, :]
        off = jnp.concatenate(offs, axis=0)
        w = jnp.concatenate(
            [beta[rr * SUBLANES:(rr + 1) * SUBLANES, :] * (parts[rr] * off)
             for rr in range(seg)], axis=0).astype(BF16)
        vt = vt_ref[0, :, pl.ds(k0, blk)]
        acc0 = acc0 + jnp.dot(vt[0:HEAD_DIM, :], w[:, 0:blk], preferred_element_type=F32)
        acc1 = acc1 + jnp.dot(vt[HEAD_DIM:, :], w[:, blk:], preferred_element_type=F32)
        return c, acc0, acc1

    ones = jnp.ones((1, width), F32)
    zero = jnp.zeros((HEAD_DIM, blk), F32)
    state = (jnp.int32(0),)
    for u in range(tiles):
        state = state + block(u, first_tile + u, ones, zero, zero, True)

    def live_carries(st):
        return [jnp.where(st[0] < first_tile + u, st[1 + 3 * u], 0.0) for u in range(tiles)]

    def cond(st):
        return jnp.max(functools.reduce(jnp.maximum, live_carries(st))) > 0.0

    def body(st):
        new = (st[0] + 1,)
        for u, carry in enumerate(live_carries(st)):
            kb = jnp.maximum(first_tile + u - 1 - st[0], 0)
            new = new + block(u, kb, carry, st[2 + 3 * u], st[3 + 3 * u], False)
        return new

    state = lax.while_loop(cond, body, state)
    for u in range(tiles):
        o_ref[0, 0:HEAD_DIM, u * blk:(u + 1) * blk] = state[2 + 3 * u]
        o_ref[0, HEAD_DIM:, u * blk:(u + 1) * blk] = state[3 + 3 * u]


def _causal_bias(blk):
    seg = blk // SUBLANES
    row = jnp.arange(blk, dtype=jnp.int32)[:, None]
    key_local = (row % SUBLANES) * seg + row // SUBLANES
    query_local = jnp.arange(2 * blk, dtype=jnp.int32)[None, :] % blk
    return jnp.where(key_local < query_local, 0.0, -1e30).astype(F32)


def _stick_breaking_attention(qt_pad, k_perm, vt_perm, seq_len):
    batch, n_heads = qt_pad.shape[0], qt_pad.shape[1]
    d = n_heads * HEAD_DIM
    blk = min(ATTN_BLOCK, seq_len)
    tiles = min(ATTN_TILES, seq_len // blk)
    assert seq_len % (blk * tiles) == 0
    k3 = k_perm.reshape(batch, seq_len, d)
    return pl.pallas_call(
        functools.partial(_attn_kernel, blk=blk, tiles=tiles),
        grid=(batch, n_heads // 2, seq_len // (blk * tiles)),
        in_specs=[
            pl.BlockSpec((1, 2, 2 * HEAD_DIM, blk * tiles), lambda b, p, q: (b, p, 0, q)),
            pl.BlockSpec((1, seq_len, 2 * HEAD_DIM), lambda b, p, q: (b, 0, p)),
            pl.BlockSpec((1, 2 * HEAD_DIM, seq_len), lambda b, p, q: (b, p, 0)),
            pl.BlockSpec((blk, 2 * blk), lambda b, p, q: (0, 0)),
        ],
        out_specs=pl.BlockSpec((1, 2 * HEAD_DIM, blk * tiles), lambda b, p, q: (b, p, q)),
        out_shape=jax.ShapeDtypeStruct((batch, d, seq_len), F32),
        compiler_params=_params("arbitrary", "arbitrary", "arbitrary"),
        name="stick_breaking_attention",
    )(qt_pad, k3, vt_perm, _causal_bias(blk))


def _out_proj_kernel(ot_ref, w_ref, h_ref, o_ref):
    a = ot_ref[0].T.astype(BF16)
    o_ref[...] = h_ref[...] + jnp.dot(a, w_ref[...], preferred_element_type=F32)


def _out_proj(ot, w_o, h, seq_len):
    n, d = h.shape
    t = min(PROJ_TOKENS, seq_len)
    tiles = seq_len // t
    return pl.pallas_call(
        _out_proj_kernel,
        grid=(n // t,),
        in_specs=[
            pl.BlockSpec((1, d, t), lambda i: (i // tiles, 0, i % tiles)),
            pl.BlockSpec((d, d), lambda i: (0, 0)),
            pl.BlockSpec((t, d), lambda i: (i, 0)),
        ],
        out_specs=pl.BlockSpec((t, d), lambda i: (i, 0)),
        out_shape=jax.ShapeDtypeStruct((n, d), F32),
        compiler_params=_params("arbitrary"),
        name="attn_out_proj",
    )(ot, w_o.astype(BF16), h)


def kernel(x, norm_mix, norm_ffn, conv_w_in, conv_kernel, conv_w_out, norm_kv, w_kv,
           attn_w_q, attn_w_o, peer_w_query, peer_subkeys, peer_u, peer_v, norm_final):
    batch, seq_len, d = x.shape
    assert norm_mix.shape[0] == 2 and conv_w_in.shape[0] == 1 and attn_w_q.shape[0] == 1
    h = x.reshape(batch * seq_len, d)
    h = _conv_mixer(h, norm_mix[0], conv_w_in[0], conv_kernel[0], conv_w_out[0], seq_len)
    h = _peer_ffn(h, norm_ffn[0], peer_w_query[0], peer_subkeys[0], peer_u[0], peer_v[0],
                  norm_final, final_norm=False)
    k_perm, vt_perm, qt_pad = _kvq_proj(h, norm_kv, norm_mix[1], w_kv, attn_w_q[0],
                                        batch, seq_len)
    ot = _stick_breaking_attention(qt_pad, k_perm, vt_perm, seq_len)
    h = _out_proj(ot, attn_w_o[0], h, seq_len)
    h = _peer_ffn(h, norm_ffn[1], peer_w_query[1], peer_subkeys[1], peer_u[1], peer_v[1],
                  norm_final, final_norm=True)
    return h.reshape(batch, seq_len, d)
```

```python
import functools
import math

import jax
import jax.numpy as jnp
from jax import lax
from jax.experimental import pallas as pl
from jax.experimental.pallas import tpu as pltpu

F32 = jnp.float32
BF16 = jnp.bfloat16

RMS_EPS = 1e-6
CONV_WIDTH = 3
HEAD_DIM = 64
PEER_HEADS = 8
PEER_N_KEYS = 128
PEER_KEY_DIM = 128
PEER_TOPK = 16

SUBLANES = 8
LANES = 128
VMEM_LIMIT_BYTES = 56 * 1024 * 1024

CONV_TOKENS = 512
TOPK_TOKENS = SUBLANES * LANES
SCORE_PITCH = PEER_N_KEYS + SUBLANES
TOPK_CHAINS = 4
GATE_TOKENS = LANES
GATE_UNROLL = 128
EXPERT_TOKENS = 1024
EXPERT_ROWS = 2 * SUBLANES
EXPERT_BLOCK = EXPERT_ROWS * PEER_N_KEYS
N_EXPERT_BLOCKS = PEER_N_KEYS // EXPERT_ROWS
EXPERT_CHUNK = 2 * PEER_N_KEYS
PROJ_TOKENS = 512
ATTN_BLOCK = 128
ATTN_TILES = 16

_NT = (((1,), (1,)), ((), ()))


def _params(*sem):
    return pltpu.CompilerParams(dimension_semantics=sem,
                                vmem_limit_bytes=VMEM_LIMIT_BYTES)


def _rms_scale(x):
    return x * lax.rsqrt(jnp.mean(x * x, axis=-1, keepdims=True) + RMS_EPS)


def _conv_mixer_kernel(x_ref, g_ref, win_ref, ck_ref, wout_ref, o_ref, ubuf_ref,
                       *, tiles_per_seq):
    t = x_ref.shape[0]
    d = x_ref.shape[1]
    halo = SUBLANES
    first = (pl.program_id(0) % tiles_per_seq) == 0

    @pl.when(first)
    def _():
        ubuf_ref[0:halo, :] = jnp.zeros((halo, d), F32)

    @pl.when(jnp.logical_not(first))
    def _():
        ubuf_ref[0:halo, :] = ubuf_ref[t:t + halo, :]

    x = x_ref[...]
    xb = (_rms_scale(x) * g_ref[...]).astype(BF16)
    bg = jnp.dot(xb, win_ref[:, 0:d], preferred_element_type=F32)
    cg = jnp.dot(xb, win_ref[:, d:2 * d], preferred_element_type=F32)
    hv = jnp.dot(xb, win_ref[:, 2 * d:3 * d], preferred_element_type=F32)
    u = cg * hv
    ubuf_ref[halo:halo + t, :] = u
    conv = ubuf_ref[halo - 2:halo - 2 + t, :] * ck_ref[0:1, :]
    conv = conv + ubuf_ref[halo - 1:halo - 1 + t, :] * ck_ref[1:2, :]
    conv = conv + u * ck_ref[2:3, :]
    y = (bg * conv).astype(BF16)
    o_ref[...] = x + jnp.dot(y, wout_ref[...], preferred_element_type=F32)


def _conv_mixer(h, gain, w_in, conv_k, w_out, seq_len):
    n, d = h.shape
    t = min(CONV_TOKENS, seq_len)
    assert seq_len % t == 0 and conv_k.shape[0] == CONV_WIDTH
    return pl.pallas_call(
        functools.partial(_conv_mixer_kernel, tiles_per_seq=seq_len // t),
        grid=(n // t,),
        in_specs=[
            pl.BlockSpec((t, d), lambda i: (i, 0)),
            pl.BlockSpec((1, d), lambda i: (0, 0)),
            pl.BlockSpec((d, 3 * d), lambda i: (0, 0)),
            pl.BlockSpec((CONV_WIDTH, d), lambda i: (0, 0)),
            pl.BlockSpec((d, d), lambda i: (0, 0)),
        ],
        out_specs=pl.BlockSpec((t, d), lambda i: (i, 0)),
        out_shape=jax.ShapeDtypeStruct((n, d), F32),
        scratch_shapes=[pltpu.VMEM((t + 2 * SUBLANES, d), F32)],
        compiler_params=_params("arbitrary"),
        name="conv_mixer",
    )(h, gain.reshape(1, d), w_in.astype(BF16), conv_k, w_out.astype(BF16))


def _top_planes(planes, ids, k):
    planes = list(planes)
    chunk = -(-len(planes) // TOPK_CHAINS)
    vals, picked = [], []
    for it in range(k):
        best = best_id = None
        for lo in range(0, len(planes), chunk):
            part = planes[lo]
            part_id = jnp.full(part.shape, ids[lo], F32)
            for plane, plane_id in zip(planes[lo + 1:lo + chunk], ids[lo + 1:lo + chunk]):
                better = plane > part
                part = jnp.where(better, plane, part)
                part_id = jnp.where(better, plane_id, part_id)
            if best is None:
                best, best_id = part, part_id
            else:
                better = part > best
                best = jnp.where(better, part, best)
                best_id = jnp.where(better, part_id, best_id)
        vals.append(best)
        picked.append(best_id)
        if it + 1 < k:
            planes = [jnp.where(best_id == plane_id, -jnp.inf, plane)
                      for plane, plane_id in zip(planes, ids)]
    return vals, picked


def _sort_network(n):
    pairs = []

    def merge(lo, count, stride):
        step = stride * 2
        if step < count:
            merge(lo, count, step)
            merge(lo + stride, count, step)
            pairs.extend((i, i + stride) for i in range(lo + stride, lo + count - stride, step))
        else:
            pairs.append((lo, lo + stride))

    def sort(lo, count):
        if count > 1:
            sort(lo, count // 2)
            sort(lo + count // 2, count // 2)
            merge(lo, count, 1)

    sort(0, n)
    return pairs


def _order(vals, ids, i, j):
    swap = vals[j] > vals[i]
    vals[i], vals[j] = jnp.maximum(vals[i], vals[j]), jnp.minimum(vals[i], vals[j])
    ids[i], ids[j] = jnp.where(swap, ids[j], ids[i]), jnp.where(swap, ids[i], ids[j])


def _top_planes_sorted(planes, ids, k):
    assert len(planes) % k == 0 and k & (k - 1) == 0
    network = _sort_network(k)
    groups = []
    for lo in range(0, len(planes), k):
        vals = list(planes[lo:lo + k])
        gids = [jnp.full(vals[0].shape, i, F32) for i in ids[lo:lo + k]]
        for i, j in network:
            _order(vals, gids, i, j)
        groups.append((vals, gids))
    return _merge_runs(groups, planes, k)


def _bitonic_merge(vals, ids):
    dist = len(vals) // 2
    while dist:
        for i in range(len(vals)):
            if i & dist == 0:
                _order(vals, ids, i, i + dist)
        dist //= 2


def _merge_runs(runs, planes, k):
    while len(runs) > 1:
        merged = [runs[-1]] if len(runs) % 2 else []
        for (va, ia), (vb, ib) in zip(runs[0:-1:2], runs[1::2]):
            take_b = [vb[k - 1 - i] > va[i] for i in range(k)]
            vals = [jnp.maximum(va[i], vb[k - 1 - i]) for i in range(k)]
            ids = [jnp.where(t, ib[k - 1 - i], ia[i]) for i, t in enumerate(take_b)]
            _bitonic_merge(vals, ids)
            merged.append((vals, ids))
        runs = merged
    vals, ids = runs[0]
    tied = jnp.zeros(vals[0].shape, F32)
    for i in range(k - 1):
        tied = jnp.where(vals[i] > vals[i + 1], tied, 1.0)
    reach = jnp.zeros(vals[0].shape, F32)
    for plane in planes:
        reach = reach + jnp.where(plane >= vals[k - 1], 1.0, 0.0)
    tied = jnp.where(reach == float(k), tied, 1.0)
    return vals, ids, tied


def _top_sums_sorted(cands, cand_ids, k):
    half = k // 2
    as_run = lambda lo, hi: (list(cands[lo:hi]), [jnp.full(cands[0].shape, i, F32)
                                                  for i in cand_ids[lo:hi]])
    runs = [as_run(0, k)]
    shorts = [as_run(lo, lo + half) for lo in range(k, len(cands), half)]
    assert len(shorts) % 2 == 0
    for (va, ia), (vb, ib) in zip(shorts[0::2], shorts[1::2]):
        vals, ids = va + vb[::-1], ia + ib[::-1]
        _bitonic_merge(vals, ids)
        runs.append((vals, ids))
    return _merge_runs(runs, cands, k)


def _lookup(table, sel):
    out = table[0]
    for a, plane in enumerate(table[1:], start=1):
        out = jnp.where(sel == float(a), plane, out)
    return out


def _score_planes(scores, scratch_ref):
    for g in range(SUBLANES):
        scratch_ref[g * SCORE_PITCH:g * SCORE_PITCH + PEER_N_KEYS, :] = (
            scores[:, g * LANES:(g + 1) * LANES])
    return [scratch_ref[pl.ds(n, SUBLANES, stride=SCORE_PITCH), :] for n in range(PEER_N_KEYS)]


def _peer_topk_kernel(h_ref, g_ref, wqt_ref, sk_ref, xn_ref, picks_ref, s0_ref, s1_ref):
    k = PEER_TOPK
    half = k // 2
    xn_ref[...] = (_rms_scale(h_ref[...]) * g_ref[...]).astype(BF16)
    key_ids = [float(n) for n in range(PEER_N_KEYS)]
    cand_ab = ([(0, b) for b in range(k)]
               + [(a, b) for a in range(1, half) for b in range(half)]
               + [(a, 0) for a in range(half, k)])
    cand_ids = [float(a * k + b) for a, b in cand_ab]

    def head_body(hd, carry):
        q0 = pl.multiple_of(hd * 2 * PEER_KEY_DIM, 2 * PEER_KEY_DIM)
        qt = lax.dot_general(wqt_ref[pl.ds(q0, 2 * PEER_KEY_DIM), :], xn_ref[...], _NT,
                             preferred_element_type=F32).astype(BF16)
        planes = []
        for p, scratch_ref in enumerate((s0_ref, s1_ref)):
            scores = jnp.dot(sk_ref[hd, p], qt[p * PEER_KEY_DIM:(p + 1) * PEER_KEY_DIM, :],
                             preferred_element_type=F32)
            planes.append(_score_planes(scores, scratch_ref))
        v1, i1, tied1 = _top_planes_sorted(planes[0], key_ids, k)
        v2, i2, tied2 = _top_planes_sorted(planes[1], key_ids, k)

        def exact_scan():
            return _top_planes(planes[0], key_ids, k) + _top_planes(planes[1], key_ids, k)

        v1, i1, v2, i2 = lax.cond(jnp.max(jnp.maximum(tied1, tied2)) > 0.0,
                                  exact_scan, lambda: (v1, i1, v2, i2))
        cands = [v1[a] + v2[b] for a, b in cand_ab]
        top_s, top_c, tied = _top_sums_sorted(cands, cand_ids, k)
        top_s, top_c = lax.cond(jnp.max(tied) > 0.0,
                                lambda: _top_planes(cands, cand_ids, k),
                                lambda: (top_s, top_c))
        denom = None
        gates = []
        for j in range(k):
            gates.append(jnp.exp(top_s[j] - top_s[0]))
            denom = gates[j] if denom is None else denom + gates[j]
        for j in range(k):
            a_sel = jnp.floor(top_c[j] * (1.0 / k))
            b_sel = top_c[j] - a_sel * k
            row0 = pl.multiple_of((hd * k + j) * SUBLANES, SUBLANES)
            picks_ref[0, 0, pl.ds(row0, SUBLANES), :] = _lookup(i1, a_sel)
            picks_ref[0, 1, pl.ds(row0, SUBLANES), :] = _lookup(i2, b_sel)
            picks_ref[0, 2, pl.ds(row0, SUBLANES), :] = gates[j] / denom
        return carry

    lax.fori_loop(0, PEER_HEADS, head_body, 0)


def _peer_topk(h, gain, w_query, subkeys):
    n, d = h.shape
    t = TOPK_TOKENS
    qdim = PEER_HEADS * 2 * PEER_KEY_DIM
    n_picks = PEER_HEADS * PEER_TOPK
    assert subkeys.shape == (PEER_HEADS, 2, PEER_N_KEYS, PEER_KEY_DIM)
    assert w_query.shape == (d, qdim) and n % t == 0
    return pl.pallas_call(
        _peer_topk_kernel,
        grid=(n // t,),
        in_specs=[
            pl.BlockSpec((t, d), lambda i: (i, 0)),
            pl.BlockSpec((1, d), lambda i: (0, 0)),
            pl.BlockSpec((qdim, d), lambda i: (0, 0)),
            pl.BlockSpec((PEER_HEADS, 2, PEER_N_KEYS, PEER_KEY_DIM), lambda i: (0, 0, 0, 0)),
        ],
        out_specs=[
            pl.BlockSpec((t, d), lambda i: (i, 0)),
            pl.BlockSpec((1, 3, n_picks * SUBLANES, LANES), lambda i: (i, 0, 0, 0)),
        ],
        out_shape=[
            jax.ShapeDtypeStruct((n, d), BF16),
            jax.ShapeDtypeStruct((n // t, 3, n_picks * SUBLANES, LANES), F32),
        ],
        scratch_shapes=[pltpu.VMEM((SUBLANES * SCORE_PITCH, LANES), F32)] * 2,
        compiler_params=_params("arbitrary"),
        name="peer_topk",
    )(h, gain.reshape(1, d), w_query.T.astype(BF16), subkeys.astype(BF16))


def _unpack_bf16_pair(words, which):
    if which == 0:
        return pltpu.bitcast(lax.shift_left(words, jnp.uint32(16)), F32)
    return pltpu.bitcast(words & jnp.uint32(0xFFFF0000), F32)


def _gate_build_kernel(picks_ref, gate_ref, i1t_s, i2t_s, gtt_s):
    n_picks = picks_ref.shape[2] // SUBLANES
    group = pl.program_id(1)
    for which, dst in enumerate((i1t_s, i2t_s, gtt_s)):
        dst[...] = picks_ref[0, which, pl.ds(group, n_picks, stride=SUBLANES), :].T
    sub_id = lax.broadcasted_iota(jnp.int32, (PEER_N_KEYS, n_picks), 0).astype(F32)

    def token_body(c, carry):
        i1r = i1t_s[pl.ds(c, 1), :]
        i2r = i2t_s[pl.ds(c, 1), :]
        gr = gtt_s[pl.ds(c, 1), :]
        a = jnp.where(sub_id == i1r, gr, 0.0).astype(BF16)
        b = jnp.where(sub_id == i2r, 1.0, 0.0).astype(BF16)
        g = lax.dot_general(a, b, _NT, preferred_element_type=F32).astype(BF16)
        for blk in range(N_EXPERT_BLOCKS):
            gate_ref[blk, c] = pltpu.bitcast(g[blk * EXPERT_ROWS:(blk + 1) * EXPERT_ROWS, :],
                                             jnp.uint32)
        return carry

    lax.fori_loop(0, GATE_TOKENS, token_body, 0, unroll=GATE_UNROLL)


def _gate_build(picks):
    tiles, _, pick_rows, _ = picks.shape
    n = tiles * TOPK_TOKENS
    n_picks = pick_rows // SUBLANES
    return pl.pallas_call(
        _gate_build_kernel,
        grid=(tiles, SUBLANES),
        in_specs=[pl.BlockSpec((1, 3, pick_rows, LANES), lambda i, g: (i, 0, 0, 0))],
        out_specs=pl.BlockSpec((N_EXPERT_BLOCKS, GATE_TOKENS, SUBLANES, PEER_N_KEYS),
                               lambda i, g: (0, i * SUBLANES + g, 0, 0)),
        out_shape=jax.ShapeDtypeStruct((N_EXPERT_BLOCKS, n, SUBLANES, PEER_N_KEYS), jnp.uint32),
        scratch_shapes=[pltpu.VMEM((GATE_TOKENS, n_picks), F32)] * 3,
        compiler_params=_params("arbitrary", "arbitrary"),
        name="peer_gate_build",
    )(picks)


def _peer_expert_kernel(xn_ref, gate_ref, u_ref, v_ref, h_ref, fg_ref, o_ref, coef_ref,
                        *, final_norm):
    e = pl.program_id(1)
    t = xn_ref.shape[0]

    @pl.when(e == 0)
    def _():
        o_ref[...] = h_ref[...]

    xn = xn_ref[...]
    per_chunk = EXPERT_CHUNK // PEER_N_KEYS
    for q in range(EXPERT_BLOCK // EXPERT_CHUNK):
        cols = slice(q * EXPERT_CHUNK, (q + 1) * EXPERT_CHUNK)
        act = lax.dot_general(xn, u_ref[cols, :], _NT, preferred_element_type=F32)
        gelu = 0.5 * act * (1.0 + lax.erf(act * (1.0 / math.sqrt(2.0))))
        assert per_chunk == 2
        words = gate_ref[0, pl.ds(q, t, stride=SUBLANES), :]
        gate = jnp.concatenate([_unpack_bf16_pair(words, 0), _unpack_bf16_pair(words, 1)], axis=1)
        coef_ref[:, cols] = (gate * gelu).astype(BF16)
    o_ref[...] += jnp.dot(coef_ref[...], v_ref[...], preferred_element_type=F32)

    if final_norm:
        @pl.when(e == pl.num_programs(1) - 1)
        def _():
            o_ref[...] = _rms_scale(o_ref[...]) * fg_ref[...]


def _peer_experts(xn, gate, u_tbl, v_tbl, h, final_gain, final_norm):
    n, d = h.shape
    t = min(EXPERT_TOKENS, n)
    assert n % t == 0 and u_tbl.shape == (N_EXPERT_BLOCKS * EXPERT_BLOCK, d)
    gate2 = gate.reshape(gate.shape[0], n * SUBLANES, PEER_N_KEYS)
    return pl.pallas_call(
        functools.partial(_peer_expert_kernel, final_norm=final_norm),
        grid=(n // t, N_EXPERT_BLOCKS),
        in_specs=[
            pl.BlockSpec((t, d), lambda i, e: (i, 0)),
            pl.BlockSpec((1, t * SUBLANES, PEER_N_KEYS), lambda i, e: (e, i, 0)),
            pl.BlockSpec((EXPERT_BLOCK, d), lambda i, e: (e, 0)),
            pl.BlockSpec((EXPERT_BLOCK, d), lambda i, e: (e, 0)),
            pl.BlockSpec((t, d), lambda i, e: (i, 0)),
            pl.BlockSpec((1, d), lambda i, e: (0, 0)),
        ],
        out_specs=pl.BlockSpec((t, d), lambda i, e: (i, 0)),
        out_shape=jax.ShapeDtypeStruct((n, d), F32),
        scratch_shapes=[pltpu.VMEM((t, EXPERT_BLOCK), BF16)],
        compiler_params=_params("arbitrary", "arbitrary"),
        name="peer_experts",
    )(xn, gate2, u_tbl.astype(BF16), v_tbl.astype(BF16), h, final_gain.reshape(1, d))


def _peer_ffn(h, gain, w_query, subkeys, u_tbl, v_tbl, final_gain, final_norm):
    xn, picks = _peer_topk(h, gain, w_query, subkeys)
    return _peer_experts(xn, _gate_build(picks), u_tbl, v_tbl, h, final_gain, final_norm)


def _kvq_kernel(h_ref, hp_ref, gkv_ref, gq_ref, wk_ref, wvt_ref, wqt_ref,
                k_ref, vt_ref, qt_ref):
    n_heads = qt_ref.shape[1]
    t = h_ref.shape[0]
    xkv = (_rms_scale(hp_ref[...]) * gkv_ref[...]).astype(BF16)
    xq = (_rms_scale(h_ref[...]) * gq_ref[...]).astype(BF16)
    k_ref[...] = jnp.dot(xkv, wk_ref[...], preferred_element_type=F32).astype(BF16)
    vt_ref[0] = lax.dot_general(wvt_ref[...], xkv, _NT,
                                preferred_element_type=F32).astype(BF16)
    qt = (lax.dot_general(wqt_ref[...], xq, _NT, preferred_element_type=F32)
          * (0.5 * HEAD_DIM ** -0.5)).astype(BF16)
    zeros = jnp.zeros((HEAD_DIM, t), BF16)
    for hd in range(n_heads):
        q_h = qt[hd * HEAD_DIM:(hd + 1) * HEAD_DIM, :]
        if hd % 2 == 0:
            qt_ref[0, hd, 0:HEAD_DIM, :] = q_h
            qt_ref[0, hd, HEAD_DIM:2 * HEAD_DIM, :] = zeros
        else:
            qt_ref[0, hd, 0:HEAD_DIM, :] = zeros
            qt_ref[0, hd, HEAD_DIM:2 * HEAD_DIM, :] = q_h


def _key_permute(x, seq_len):
    n, d = x.shape
    kb = min(ATTN_BLOCK, seq_len)
    seg = kb // SUBLANES
    return x.reshape(n // kb, SUBLANES, seg, d).swapaxes(1, 2).reshape(n, d)


def _kvq_proj(h, g_kv, g_q, w_kv, w_q, batch, seq_len):
    n, d = h.shape
    n_heads = d // HEAD_DIM
    t = min(PROJ_TOKENS, seq_len)
    tiles = seq_len // t
    assert seq_len % t == 0 and t % min(ATTN_BLOCK, seq_len) == 0 and n_heads % 2 == 0
    w_k, w_v = w_kv[:, :d], w_kv[:, d:]
    return pl.pallas_call(
        _kvq_kernel,
        grid=(n // t,),
        in_specs=[
            pl.BlockSpec((t, d), lambda i: (i, 0)),
            pl.BlockSpec((t, d), lambda i: (i, 0)),
            pl.BlockSpec((1, d), lambda i: (0, 0)),
            pl.BlockSpec((1, d), lambda i: (0, 0)),
            pl.BlockSpec((d, d), lambda i: (0, 0)),
            pl.BlockSpec((d, d), lambda i: (0, 0)),
            pl.BlockSpec((d, d), lambda i: (0, 0)),
        ],
        out_specs=[
            pl.BlockSpec((t, d), lambda i: (i, 0)),
            pl.BlockSpec((1, d, t), lambda i: (i // tiles, 0, i % tiles)),
            pl.BlockSpec((1, n_heads, 2 * HEAD_DIM, t), lambda i: (i // tiles, 0, 0, i % tiles)),
        ],
        out_shape=[
            jax.ShapeDtypeStruct((n, d), BF16),
            jax.ShapeDtypeStruct((batch, d, seq_len), BF16),
            jax.ShapeDtypeStruct((batch, n_heads, 2 * HEAD_DIM, seq_len), BF16),
        ],
        compiler_params=_params("arbitrary"),
        name="kvq_proj",
    )(h, _key_permute(h, seq_len), g_kv.reshape(1, d), g_q.reshape(1, d),
      w_k.astype(BF16), w_v.T.astype(BF16), w_q.T.astype(BF16))


def _attn_kernel(qt_ref, k_ref, vt_ref, bias_ref, o_ref, *, blk, tiles):
    seg = blk // SUBLANES
    width = 2 * blk
    first_tile = pl.program_id(2) * tiles
    qts = [jnp.concatenate([qt_ref[0, 0, :, u * blk:(u + 1) * blk],
                            qt_ref[0, 1, :, u * blk:(u + 1) * blk]], axis=1)
           for u in range(tiles)]

    def block(u, kb, carry, acc0, acc1, masked):
        k0 = pl.multiple_of(kb * blk, blk)
        zh = jnp.dot(k_ref[0, pl.ds(k0, blk), :], qts[u], preferred_element_type=F32)
        if masked:
            zh = zh + bias_ref[...]
        ht = 0.5 * jnp.tanh(zh)
        beta = 0.5 + ht
        keep = 0.5 - ht
        run = jnp.ones((SUBLANES, width), F32)
        parts = [None] * seg
        for rr in reversed(range(seg)):
            parts[rr] = run
            run = run * keep[rr * SUBLANES:(rr + 1) * SUBLANES, :]
        offs = [None] * SUBLANES
        c = carry
        for s in reversed(range(SUBLANES)):
            offs[s] = c
            c = c * run[s:s + 1, :]
        off = jnp.concatenate(offs, axis=0)
        w = jnp.concatenate(
            [beta[rr * SUBLANES:(rr + 1) * SUBLANES, :] * (parts[rr] * off)
             for rr in range(seg)], axis=0).astype(BF16)
        vt = vt_ref[0, :, pl.ds(k0, blk)]
        acc0 = acc0 + jnp.dot(vt[0:HEAD_DIM, :], w[:, 0:blk], preferred_element_type=F32)
        acc1 = acc1 + jnp.dot(vt[HEAD_DIM:, :], w[:, blk:], preferred_element_type=F32)
        return c, acc0, acc1

    ones = jnp.ones((1, width), F32)
    zero = jnp.zeros((HEAD_DIM, blk), F32)
    state = (jnp.int32(0),)
    for u in range(tiles):
        state = state + block(u, first_tile + u, ones, zero, zero, True)

    def live_carries(st):
        return [jnp.where(st[0] < first_tile + u, st[1 + 3 * u], 0.0) for u in range(tiles)]

    def cond(st):
        return jnp.max(functools.reduce(jnp.maximum, live_carries(st))) > 0.0

    def body(st):
        new = (st[0] + 1,)
        for u, carry in enumerate(live_carries(st)):
            kb = jnp.maximum(first_tile + u - 1 - st[0], 0)
            new = new + block(u, kb, carry, st[2 + 3 * u], st[3 + 3 * u], False)
        return new

    state = lax.while_loop(cond, body, state)
    for u in range(tiles):
        o_ref[0, 0:HEAD_DIM, u * blk:(u + 1) * blk] = state[2 + 3 * u]
        o_ref[0, HEAD_DIM:, u * blk:(u + 1) * blk] = state[3 + 3 * u]


def _causal_bias(blk):
    seg = blk // SUBLANES
    row = jnp.arange(blk, dtype=jnp.int32)[:, None]
    key_local = (row % SUBLANES) * seg + row // SUBLANES
    query_local = jnp.arange(2 * blk, dtype=jnp.int32)[None, :] % blk
    return jnp.where(key_local < query_local, 0.0, -1e30).astype(F32)


def _stick_breaking_attention(qt_pad, k_perm, vt_perm, seq_len):
    batch, n_heads = qt_pad.shape[0], qt_pad.shape[1]
    d = n_heads * HEAD_DIM
    blk = min(ATTN_BLOCK, seq_len)
    tiles = min(ATTN_TILES, seq_len // blk)
    assert seq_len % (blk * tiles) == 0
    k3 = k_perm.reshape(batch, seq_len, d)
    return pl.pallas_call(
        functools.partial(_attn_kernel, blk=blk, tiles=tiles),
        grid=(batch, n_heads // 2, seq_len // (blk * tiles)),
        in_specs=[
            pl.BlockSpec((1, 2, 2 * HEAD_DIM, blk * tiles), lambda b, p, q: (b, p, 0, q)),
            pl.BlockSpec((1, seq_len, 2 * HEAD_DIM), lambda b, p, q: (b, 0, p)),
            pl.BlockSpec((1, 2 * HEAD_DIM, seq_len), lambda b, p, q: (b, p, 0)),
            pl.BlockSpec((blk, 2 * blk), lambda b, p, q: (0, 0)),
        ],
        out_specs=pl.BlockSpec((1, 2 * HEAD_DIM, blk * tiles), lambda b, p, q: (b, p, q)),
        out_shape=jax.ShapeDtypeStruct((batch, d, seq_len), F32),
        compiler_params=_params("arbitrary", "arbitrary", "arbitrary"),
        name="stick_breaking_attention",
    )(qt_pad, k3, vt_perm, _causal_bias(blk))


def _out_proj_kernel(ot_ref, w_ref, h_ref, o_ref):
    a = ot_ref[0].T.astype(BF16)
    o_ref[...] = h_ref[...] + jnp.dot(a, w_ref[...], preferred_element_type=F32)


def _out_proj(ot, w_o, h, seq_len):
    n, d = h.shape
    t = min(PROJ_TOKENS, seq_len)
    tiles = seq_len // t
    return pl.pallas_call(
        _out_proj_kernel,
        grid=(n // t,),
        in_specs=[
            pl.BlockSpec((1, d, t), lambda i: (i // tiles, 0, i % tiles)),
            pl.BlockSpec((d, d), lambda i: (0, 0)),
            pl.BlockSpec((t, d), lambda i: (i, 0)),
        ],
        out_specs=pl.BlockSpec((t, d), lambda i: (i, 0)),
        out_shape=jax.ShapeDtypeStruct((n, d), F32),
        compiler_params=_params("arbitrary"),
        name="attn_out_proj",
    )(ot, w_o.astype(BF16), h)


def kernel(x, norm_mix, norm_ffn, conv_w_in, conv_kernel, conv_w_out, norm_kv, w_kv,
           attn_w_q, attn_w_o, peer_w_query, peer_subkeys, peer_u, peer_v, norm_final):
    batch, seq_len, d = x.shape
    assert norm_mix.shape[0] == 2 and conv_w_in.shape[0] == 1 and attn_w_q.shape[0] == 1
    h = x.reshape(batch * seq_len, d)
    h = _conv_mixer(h, norm_mix[0], conv_w_in[0], conv_kernel[0], conv_w_out[0], seq_len)
    h = _peer_ffn(h, norm_ffn[0], peer_w_query[0], peer_subkeys[0], peer_u[0], peer_v[0],
                  norm_final, final_norm=False)
    k_perm, vt_perm, qt_pad = _kvq_proj(h, norm_kv, norm_mix[1], w_kv, attn_w_q[0],
                                        batch, seq_len)
    ot = _stick_breaking_attention(qt_pad, k_perm, vt_perm, seq_len)
    h = _out_proj(ot, attn_w_o[0], h, seq_len)
    h = _peer_ffn(h, norm_ffn[1], peer_w_query[1], peer_subkeys[1], peer_u[1], peer_v[1],
                  norm_final, final_norm=True)
    return h.reshape(batch, seq_len, d)
```

```python
import functools
import math

import jax
import jax.numpy as jnp
from jax import lax
from jax.experimental import pallas as pl
from jax.experimental.pallas import tpu as pltpu

F32 = jnp.float32
BF16 = jnp.bfloat16

RMS_EPS = 1e-6
CONV_WIDTH = 3
HEAD_DIM = 64
PEER_HEADS = 8
PEER_N_KEYS = 128
PEER_KEY_DIM = 128
PEER_TOPK = 16

SUBLANES = 8
LANES = 128
VMEM_LIMIT_BYTES = 56 * 1024 * 1024

CONV_TOKENS = 512
TOPK_TOKENS = SUBLANES * LANES
SCORE_PITCH = PEER_N_KEYS + SUBLANES
TOPK_CHAINS = 4
GATE_TOKENS = LANES
GATE_UNROLL = 128
EXPERT_TOKENS = 1024
EXPERT_ROWS = 2 * SUBLANES
EXPERT_BLOCK = EXPERT_ROWS * PEER_N_KEYS
N_EXPERT_BLOCKS = PEER_N_KEYS // EXPERT_ROWS
EXPERT_CHUNK = 2 * PEER_N_KEYS
PROJ_TOKENS = 512
ATTN_BLOCK = 128
ATTN_TILES = 16

_NT = (((1,), (1,)), ((), ()))


def _params(*sem):
    return pltpu.CompilerParams(dimension_semantics=sem,
                                vmem_limit_bytes=VMEM_LIMIT_BYTES)


def _rms_scale(x):
    return x * lax.rsqrt(jnp.mean(x * x, axis=-1, keepdims=True) + RMS_EPS)


def _conv_mixer_kernel(x_ref, g_ref, win_ref, ck_ref, wout_ref, o_ref, ubuf_ref,
                       *, tiles_per_seq):
    t = x_ref.shape[0]
    d = x_ref.shape[1]
    halo = SUBLANES
    first = (pl.program_id(0) % tiles_per_seq) == 0

    @pl.when(first)
    def _():
        ubuf_ref[0:halo, :] = jnp.zeros((halo, d), F32)

    @pl.when(jnp.logical_not(first))
    def _():
        ubuf_ref[0:halo, :] = ubuf_ref[t:t + halo, :]

    x = x_ref[...]
    xb = (_rms_scale(x) * g_ref[...]).astype(BF16)
    bg = jnp.dot(xb, win_ref[:, 0:d], preferred_element_type=F32)
    cg = jnp.dot(xb, win_ref[:, d:2 * d], preferred_element_type=F32)
    hv = jnp.dot(xb, win_ref[:, 2 * d:3 * d], preferred_element_type=F32)
    u = cg * hv
    ubuf_ref[halo:halo + t, :] = u
    conv = ubuf_ref[halo - 2:halo - 2 + t, :] * ck_ref[0:1, :]
    conv = conv + ubuf_ref[halo - 1:halo - 1 + t, :] * ck_ref[1:2, :]
    conv = conv + u * ck_ref[2:3, :]
    y = (bg * conv).astype(BF16)
    o_ref[...] = x + jnp.dot(y, wout_ref[...], preferred_element_type=F32)


def _conv_mixer(h, gain, w_in, conv_k, w_out, seq_len):
    n, d = h.shape
    t = min(CONV_TOKENS, seq_len)
    assert seq_len % t == 0 and conv_k.shape[0] == CONV_WIDTH
    return pl.pallas_call(
        functools.partial(_conv_mixer_kernel, tiles_per_seq=seq_len // t),
        grid=(n // t,),
        in_specs=[
            pl.BlockSpec((t, d), lambda i: (i, 0)),
            pl.BlockSpec((1, d), lambda i: (0, 0)),
            pl.BlockSpec((d, 3 * d), lambda i: (0, 0)),
            pl.BlockSpec((CONV_WIDTH, d), lambda i: (0, 0)),
            pl.BlockSpec((d, d), lambda i: (0, 0)),
        ],
        out_specs=pl.BlockSpec((t, d), lambda i: (i, 0)),
        out_shape=jax.ShapeDtypeStruct((n, d), F32),
        scratch_shapes=[pltpu.VMEM((t + 2 * SUBLANES, d), F32)],
        compiler_params=_params("arbitrary"),
        name="conv_mixer",
    )(h, gain.reshape(1, d), w_in.astype(BF16), conv_k, w_out.astype(BF16))


def _top_planes(planes, ids, k):
    planes = list(planes)
    chunk = -(-len(planes) // TOPK_CHAINS)
    vals, picked = [], []
    for it in range(k):
        best = best_id = None
        for lo in range(0, len(planes), chunk):
            part = planes[lo]
            part_id = jnp.full(part.shape, ids[lo], F32)
            for plane, plane_id in zip(planes[lo + 1:lo + chunk], ids[lo + 1:lo + chunk]):
                better = plane > part
                part = jnp.where(better, plane, part)
                part_id = jnp.where(better, plane_id, part_id)
            if best is None:
                best, best_id = part, part_id
            else:
                better = part > best
                best = jnp.where(better, part, best)
                best_id = jnp.where(better, part_id, best_id)
        vals.append(best)
        picked.append(best_id)
        if it + 1 < k:
            planes = [jnp.where(best_id == plane_id, -jnp.inf, plane)
                      for plane, plane_id in zip(planes, ids)]
    return vals, picked


def _sort_network(n):
    pairs = []

    def merge(lo, count, stride):
        step = stride * 2
        if step < count:
            merge(lo, count, step)
            merge(lo + stride, count, step)
            pairs.extend((i, i + stride) for i in range(lo + stride, lo + count - stride, step))
        else:
            pairs.append((lo, lo + stride))

    def sort(lo, count):
        if count > 1:
            sort(lo, count // 2)
            sort(lo + count // 2, count // 2)
            merge(lo, count, 1)

    sort(0, n)
    return pairs


def _order(vals, ids, i, j):
    swap = vals[j] > vals[i]
    vals[i], vals[j] = jnp.maximum(vals[i], vals[j]), jnp.minimum(vals[i], vals[j])
    ids[i], ids[j] = jnp.where(swap, ids[j], ids[i]), jnp.where(swap, ids[i], ids[j])


def _top_planes_sorted(planes, ids, k):
    assert len(planes) % k == 0 and k & (k - 1) == 0
    network = _sort_network(k)
    groups = []
    for lo in range(0, len(planes), k):
        vals = list(planes[lo:lo + k])
        gids = [jnp.full(vals[0].shape, i, F32) for i in ids[lo:lo + k]]
        for i, j in network:
            _order(vals, gids, i, j)
        groups.append((vals, gids))
    return _merge_runs(groups, planes, k)


def _bitonic_merge(vals, ids):
    dist = len(vals) // 2
    while dist:
        for i in range(len(vals)):
            if i & dist == 0:
                _order(vals, ids, i, i + dist)
        dist //= 2


def _merge_runs(runs, planes, k):
    while len(runs) > 1:
        merged = [runs[-1]] if len(runs) % 2 else []
        for (va, ia), (vb, ib) in zip(runs[0:-1:2], runs[1::2]):
            take_b = [vb[k - 1 - i] > va[i] for i in range(k)]
            vals = [jnp.maximum(va[i], vb[k - 1 - i]) for i in range(k)]
            ids = [jnp.where(t, ib[k - 1 - i], ia[i]) for i, t in enumerate(take_b)]
            _bitonic_merge(vals, ids)
            merged.append((vals, ids))
        runs = merged
    vals, ids = runs[0]
    tied = jnp.zeros(vals[0].shape, F32)
    for i in range(k - 1):
        tied = jnp.where(vals[i] > vals[i + 1], tied, 1.0)
    hits = [jnp.where(plane >= vals[k - 1], 1.0, 0.0) for plane in planes]
    while len(hits) > 1:
        hits = [a + b for a, b in zip(hits[0::2], hits[1::2])] + hits[len(hits) & ~1:]
    tied = jnp.where(hits[0] == float(k), tied, 1.0)
    return vals, ids, tied


def _top_sums_sorted(cands, cand_ids, k):
    half = k // 2
    as_run = lambda lo, hi: (list(cands[lo:hi]), [jnp.full(cands[0].shape, i, F32)
                                                  for i in cand_ids[lo:hi]])
    runs = [as_run(0, k)]
    shorts = [as_run(lo, lo + half) for lo in range(k, len(cands), half)]
    assert len(shorts) % 2 == 0
    for (va, ia), (vb, ib) in zip(shorts[0::2], shorts[1::2]):
        vals, ids = va + vb[::-1], ia + ib[::-1]
        _bitonic_merge(vals, ids)
        runs.append((vals, ids))
    return _merge_runs(runs, cands, k)


def _lookup(table, sel):
    out = table[0]
    for a, plane in enumerate(table[1:], start=1):
        out = jnp.where(sel == float(a), plane, out)
    return out


def _score_planes(scores, scratch_ref):
    for g in range(SUBLANES):
        scratch_ref[g * SCORE_PITCH:g * SCORE_PITCH + PEER_N_KEYS, :] = (
            scores[:, g * LANES:(g + 1) * LANES])
    return [scratch_ref[pl.ds(n, SUBLANES, stride=SCORE_PITCH), :] for n in range(PEER_N_KEYS)]


def _peer_topk_kernel(h_ref, g_ref, wqt_ref, sk_ref, xn_ref, picks_ref, s0_ref, s1_ref):
    k = PEER_TOPK
    half = k // 2
    xn_ref[...] = (_rms_scale(h_ref[...]) * g_ref[...]).astype(BF16)
    key_ids = [float(n) for n in range(PEER_N_KEYS)]
    cand_ab = ([(0, b) for b in range(k)]
               + [(a, b) for a in range(1, half) for b in range(half)]
               + [(a, 0) for a in range(half, k)])
    cand_ids = [float(a * k + b) for a, b in cand_ab]

    def head_body(hd, carry):
        q0 = pl.multiple_of(hd * 2 * PEER_KEY_DIM, 2 * PEER_KEY_DIM)
        qt = lax.dot_general(wqt_ref[pl.ds(q0, 2 * PEER_KEY_DIM), :], xn_ref[...], _NT,
                             preferred_element_type=F32).astype(BF16)
        planes = []
        for p, scratch_ref in enumerate((s0_ref, s1_ref)):
            scores = jnp.dot(sk_ref[hd, p], qt[p * PEER_KEY_DIM:(p + 1) * PEER_KEY_DIM, :],
                             preferred_element_type=F32)
            planes.append(_score_planes(scores, scratch_ref))
        v1, i1, tied1 = _top_planes_sorted(planes[0], key_ids, k)
        v2, i2, tied2 = _top_planes_sorted(planes[1], key_ids, k)

        def exact_scan():
            return _top_planes(planes[0], key_ids, k) + _top_planes(planes[1], key_ids, k)

        v1, i1, v2, i2 = lax.cond(jnp.max(jnp.maximum(tied1, tied2)) > 0.0,
                                  exact_scan, lambda: (v1, i1, v2, i2))
        cands = [v1[a] + v2[b] for a, b in cand_ab]
        top_s, top_c, tied = _top_sums_sorted(cands, cand_ids, k)
        top_s, top_c = lax.cond(jnp.max(tied) > 0.0,
                                lambda: _top_planes(cands, cand_ids, k),
                                lambda: (top_s, top_c))
        denom = None
        gates = []
        for j in range(k):
            gates.append(jnp.exp(top_s[j] - top_s[0]))
            denom = gates[j] if denom is None else denom + gates[j]
        for j in range(k):
            a_sel = jnp.floor(top_c[j] * (1.0 / k))
            b_sel = top_c[j] - a_sel * k
            row0 = pl.multiple_of((hd * k + j) * SUBLANES, SUBLANES)
            picks_ref[0, 0, pl.ds(row0, SUBLANES), :] = _lookup(i1, a_sel)
            picks_ref[0, 1, pl.ds(row0, SUBLANES), :] = _lookup(i2, b_sel)
            picks_ref[0, 2, pl.ds(row0, SUBLANES), :] = gates[j] / denom
        return carry

    lax.fori_loop(0, PEER_HEADS, head_body, 0)


def _peer_topk(h, gain, w_query, subkeys):
    n, d = h.shape
    t = TOPK_TOKENS
    qdim = PEER_HEADS * 2 * PEER_KEY_DIM
    n_picks = PEER_HEADS * PEER_TOPK
    assert subkeys.shape == (PEER_HEADS, 2, PEER_N_KEYS, PEER_KEY_DIM)
    assert w_query.shape == (d, qdim) and n % t == 0
    return pl.pallas_call(
        _peer_topk_kernel,
        grid=(n // t,),
        in_specs=[
            pl.BlockSpec((t, d), lambda i: (i, 0)),
            pl.BlockSpec((1, d), lambda i: (0, 0)),
            pl.BlockSpec((qdim, d), lambda i: (0, 0)),
            pl.BlockSpec((PEER_HEADS, 2, PEER_N_KEYS, PEER_KEY_DIM), lambda i: (0, 0, 0, 0)),
        ],
        out_specs=[
            pl.BlockSpec((t, d), lambda i: (i, 0)),
            pl.BlockSpec((1, 3, n_picks * SUBLANES, LANES), lambda i: (i, 0, 0, 0)),
        ],
        out_shape=[
            jax.ShapeDtypeStruct((n, d), BF16),
            jax.ShapeDtypeStruct((n // t, 3, n_picks * SUBLANES, LANES), F32),
        ],
        scratch_shapes=[pltpu.VMEM((SUBLANES * SCORE_PITCH, LANES), F32)] * 2,
        compiler_params=_params("arbitrary"),
        name="peer_topk",
    )(h, gain.reshape(1, d), w_query.T.astype(BF16), subkeys.astype(BF16))


def _unpack_bf16_pair(words, which):
    if which == 0:
        return pltpu.bitcast(lax.shift_left(words, jnp.uint32(16)), F32)
    return pltpu.bitcast(words & jnp.uint32(0xFFFF0000), F32)


def _gate_build_kernel(picks_ref, gate_ref, i1t_s, i2t_s, gtt_s):
    n_picks = picks_ref.shape[2] // SUBLANES
    group = pl.program_id(1)
    for which, dst in enumerate((i1t_s, i2t_s, gtt_s)):
        dst[...] = picks_ref[0, which, pl.ds(group, n_picks, stride=SUBLANES), :].T
    sub_id = lax.broadcasted_iota(jnp.int32, (PEER_N_KEYS, n_picks), 0).astype(F32)

    def token_body(c, carry):
        i1r = i1t_s[pl.ds(c, 1), :]
        i2r = i2t_s[pl.ds(c, 1), :]
        gr = gtt_s[pl.ds(c, 1), :]
        a = jnp.where(sub_id == i1r, gr, 0.0).astype(BF16)
        b = jnp.where(sub_id == i2r, 1.0, 0.0).astype(BF16)
        g = lax.dot_general(a, b, _NT, preferred_element_type=F32).astype(BF16)
        for blk in range(N_EXPERT_BLOCKS):
            gate_ref[blk, c] = pltpu.bitcast(g[blk * EXPERT_ROWS:(blk + 1) * EXPERT_ROWS, :],
                                             jnp.uint32)
        return carry

    lax.fori_loop(0, GATE_TOKENS, token_body, 0, unroll=GATE_UNROLL)


def _gate_build(picks):
    tiles, _, pick_rows, _ = picks.shape
    n = tiles * TOPK_TOKENS
    n_picks = pick_rows // SUBLANES
    return pl.pallas_call(
        _gate_build_kernel,
        grid=(tiles, SUBLANES),
        in_specs=[pl.BlockSpec((1, 3, pick_rows, LANES), lambda i, g: (i, 0, 0, 0))],
        out_specs=pl.BlockSpec((N_EXPERT_BLOCKS, GATE_TOKENS, SUBLANES, PEER_N_KEYS),
                               lambda i, g: (0, i * SUBLANES + g, 0, 0)),
        out_shape=jax.ShapeDtypeStruct((N_EXPERT_BLOCKS, n, SUBLANES, PEER_N_KEYS), jnp.uint32),
        scratch_shapes=[pltpu.VMEM((GATE_TOKENS, n_picks), F32)] * 3,
        compiler_params=_params("arbitrary", "arbitrary"),
        name="peer_gate_build",
    )(picks)


def _peer_expert_kernel(xn_ref, gate_ref, u_ref, v_ref, h_ref, fg_ref, o_ref, coef_ref,
                        *, final_norm):
    e = pl.program_id(1)
    t = xn_ref.shape[0]

    @pl.when(e == 0)
    def _():
        o_ref[...] = h_ref[...]

    xn = xn_ref[...]
    per_chunk = EXPERT_CHUNK // PEER_N_KEYS
    for q in range(EXPERT_BLOCK // EXPERT_CHUNK):
        cols = slice(q * EXPERT_CHUNK, (q + 1) * EXPERT_CHUNK)
        act = lax.dot_general(xn, u_ref[cols, :], _NT, preferred_element_type=F32)
        gelu = 0.5 * act * (1.0 + lax.erf(act * (1.0 / math.sqrt(2.0))))
        assert per_chunk == 2
        words = gate_ref[0, pl.ds(q, t, stride=SUBLANES), :]
        gate = jnp.concatenate([_unpack_bf16_pair(words, 0), _unpack_bf16_pair(words, 1)], axis=1)
        coef_ref[:, cols] = (gate * gelu).astype(BF16)
    o_ref[...] += jnp.dot(coef_ref[...], v_ref[...], preferred_element_type=F32)

    if final_norm:
        @pl.when(e == pl.num_programs(1) - 1)
        def _():
            o_ref[...] = _rms_scale(o_ref[...]) * fg_ref[...]


def _peer_experts(xn, gate, u_tbl, v_tbl, h, final_gain, final_norm):
    n, d = h.shape
    t = min(EXPERT_TOKENS, n)
    assert n % t == 0 and u_tbl.shape == (N_EXPERT_BLOCKS * EXPERT_BLOCK, d)
    gate2 = gate.reshape(gate.shape[0], n * SUBLANES, PEER_N_KEYS)
    return pl.pallas_call(
        functools.partial(_peer_expert_kernel, final_norm=final_norm),
        grid=(n // t, N_EXPERT_BLOCKS),
        in_specs=[
            pl.BlockSpec((t, d), lambda i, e: (i, 0)),
            pl.BlockSpec((1, t * SUBLANES, PEER_N_KEYS), lambda i, e: (e, i, 0)),
            pl.BlockSpec((EXPERT_BLOCK, d), lambda i, e: (e, 0)),
            pl.BlockSpec((EXPERT_BLOCK, d), lambda i, e: (e, 0)),
            pl.BlockSpec((t, d), lambda i, e: (i, 0)),
            pl.BlockSpec((1, d), lambda i, e: (0, 0)),
        ],
        out_specs=pl.BlockSpec((t, d), lambda i, e: (i, 0)),
        out_shape=jax.ShapeDtypeStruct((n, d), F32),
        scratch_shapes=[pltpu.VMEM((t, EXPERT_BLOCK), BF16)],
        compiler_params=_params("arbitrary", "arbitrary"),
        name="peer_experts",
    )(xn, gate2, u_tbl.astype(BF16), v_tbl.astype(BF16), h, final_gain.reshape(1, d))


def _peer_ffn(h, gain, w_query, subkeys, u_tbl, v_tbl, final_gain, final_norm):
    xn, picks = _peer_topk(h, gain, w_query, subkeys)
    return _peer_experts(xn, _gate_build(picks), u_tbl, v_tbl, h, final_gain, final_norm)


def _kvq_kernel(h_ref, hp_ref, gkv_ref, gq_ref, wk_ref, wvt_ref, wqt_ref,
                k_ref, vt_ref, qt_ref):
    n_heads = qt_ref.shape[1]
    t = h_ref.shape[0]
    xkv = (_rms_scale(hp_ref[...]) * gkv_ref[...]).astype(BF16)
    xq = (_rms_scale(h_ref[...]) * gq_ref[...]).astype(BF16)
    k_ref[...] = jnp.dot(xkv, wk_ref[...], preferred_element_type=F32).astype(BF16)
    vt_ref[0] = lax.dot_general(wvt_ref[...], xkv, _NT,
                                preferred_element_type=F32).astype(BF16)
    qt = (lax.dot_general(wqt_ref[...], xq, _NT, preferred_element_type=F32)
          * (0.5 * HEAD_DIM ** -0.5)).astype(BF16)
    zeros = jnp.zeros((HEAD_DIM, t), BF16)
    for hd in range(n_heads):
        q_h = qt[hd * HEAD_DIM:(hd + 1) * HEAD_DIM, :]
        if hd % 2 == 0:
            qt_ref[0, hd, 0:HEAD_DIM, :] = q_h
            qt_ref[0, hd, HEAD_DIM:2 * HEAD_DIM, :] = zeros
        else:
            qt_ref[0, hd, 0:HEAD_DIM, :] = zeros
            qt_ref[0, hd, HEAD_DIM:2 * HEAD_DIM, :] = q_h


def _key_permute(x, seq_len):
    n, d = x.shape
    kb = min(ATTN_BLOCK, seq_len)
    seg = kb // SUBLANES
    return x.reshape(n // kb, SUBLANES, seg, d).swapaxes(1, 2).reshape(n, d)


def _kvq_proj(h, g_kv, g_q, w_kv, w_q, batch, seq_len):
    n, d = h.shape
    n_heads = d // HEAD_DIM
    t = min(PROJ_TOKENS, seq_len)
    tiles = seq_len // t
    assert seq_len % t == 0 and t % min(ATTN_BLOCK, seq_len) == 0 and n_heads % 2 == 0
    w_k, w_v = w_kv[:, :d], w_kv[:, d:]
    return pl.pallas_call(
        _kvq_kernel,
        grid=(n // t,),
        in_specs=[
            pl.BlockSpec((t, d), lambda i: (i, 0)),
            pl.BlockSpec((t, d), lambda i: (i, 0)),
            pl.BlockSpec((1, d), lambda i: (0, 0)),
            pl.BlockSpec((1, d), lambda i: (0, 0)),
            pl.BlockSpec((d, d), lambda i: (0, 0)),
            pl.BlockSpec((d, d), lambda i: (0, 0)),
            pl.BlockSpec((d, d), lambda i: (0, 0)),
        ],
        out_specs=[
            pl.BlockSpec((t, d), lambda i: (i, 0)),
            pl.BlockSpec((1, d, t), lambda i: (i // tiles, 0, i % tiles)),
            pl.BlockSpec((1, n_heads, 2 * HEAD_DIM, t), lambda i: (i // tiles, 0, 0, i % tiles)),
        ],
        out_shape=[
            jax.ShapeDtypeStruct((n, d), BF16),
            jax.ShapeDtypeStruct((batch, d, seq_len), BF16),
            jax.ShapeDtypeStruct((batch, n_heads, 2 * HEAD_DIM, seq_len), BF16),
        ],
        compiler_params=_params("arbitrary"),
        name="kvq_proj",
    )(h, _key_permute(h, seq_len), g_kv.reshape(1, d), g_q.reshape(1, d),
      w_k.astype(BF16), w_v.T.astype(BF16), w_q.T.astype(BF16))


def _attn_kernel(qt_ref, k_ref, vt_ref, bias_ref, o_ref, *, blk, tiles):
    seg = blk // SUBLANES
    width = 2 * blk
    first_tile = pl.program_id(2) * tiles
    qts = [jnp.concatenate([qt_ref[0, 0, :, u * blk:(u + 1) * blk],
                            qt_ref[0, 1, :, u * blk:(u + 1) * blk]], axis=1)
           for u in range(tiles)]

    def block(u, kb, carry, acc0, acc1, masked):
        k0 = pl.multiple_of(kb * blk, blk)
        zh = jnp.dot(k_ref[0, pl.ds(k0, blk), :], qts[u], preferred_element_type=F32)
        if masked:
            zh = zh + bias_ref[...]
        ht = 0.5 * jnp.tanh(zh)
        beta = 0.5 + ht
        keep = 0.5 - ht
        run = jnp.ones((SUBLANES, width), F32)
        parts = [None] * seg
        for rr in reversed(range(seg)):
            parts[rr] = run
            run = run * keep[rr * SUBLANES:(rr + 1) * SUBLANES, :]
        offs = [None] * SUBLANES
        c = carry
        for s in reversed(range(SUBLANES)):
            offs[s] = c
            c = c * run[s:s + 1, :]
        off = jnp.concatenate(offs, axis=0)
        w = jnp.concatenate(
            [beta[rr * SUBLANES:(rr + 1) * SUBLANES, :] * (parts[rr] * off)
             for rr in range(seg)], axis=0).astype(BF16)
        vt = vt_ref[0, :, pl.ds(k0, blk)]
        acc0 = acc0 + jnp.dot(vt[0:HEAD_DIM, :], w[:, 0:blk], preferred_element_type=F32)
        acc1 = acc1 + jnp.dot(vt[HEAD_DIM:, :], w[:, blk:], preferred_element_type=F32)
        return c, acc0, acc1

    ones = jnp.ones((1, width), F32)
    zero = jnp.zeros((HEAD_DIM, blk), F32)
    state = (jnp.int32(0),)
    for u in range(tiles):
        state = state + block(u, first_tile + u, ones, zero, zero, True)

    def live_carries(st):
        return [jnp.where(st[0] < first_tile + u, st[1 + 3 * u], 0.0) for u in range(tiles)]

    def cond(st):
        return jnp.max(functools.reduce(jnp.maximum, live_carries(st))) > 0.0

    def body(st):
        new = (st[0] + 1,)
        for u, carry in enumerate(live_carries(st)):
            kb = jnp.maximum(first_tile + u - 1 - st[0], 0)
            new = new + block(u, kb, carry, st[2 + 3 * u], st[3 + 3 * u], False)
        return new

    state = lax.while_loop(cond, body, state)
    for u in range(tiles):
        o_ref[0, 0:HEAD_DIM, u * blk:(u + 1) * blk] = state[2 + 3 * u]
        o_ref[0, HEAD_DIM:, u * blk:(u + 1) * blk] = state[3 + 3 * u]


def _causal_bias(blk):
    seg = blk // SUBLANES
    row = jnp.arange(blk, dtype=jnp.int32)[:, None]
    key_local = (row % SUBLANES) * seg + row // SUBLANES
    query_local = jnp.arange(2 * blk, dtype=jnp.int32)[None, :] % blk
    return jnp.where(key_local < query_local, 0.0, -1e30).astype(F32)


def _stick_breaking_attention(qt_pad, k_perm, vt_perm, seq_len):
    batch, n_heads = qt_pad.shape[0], qt_pad.shape[1]
    d = n_heads * HEAD_DIM
    blk = min(ATTN_BLOCK, seq_len)
    tiles = min(ATTN_TILES, seq_len // blk)
    assert seq_len % (blk * tiles) == 0
    k3 = k_perm.reshape(batch, seq_len, d)
    return pl.pallas_call(
        functools.partial(_attn_kernel, blk=blk, tiles=tiles),
        grid=(batch, n_heads // 2, seq_len // (blk * tiles)),
        in_specs=[
            pl.BlockSpec((1, 2, 2 * HEAD_DIM, blk * tiles), lambda b, p, q: (b, p, 0, q)),
            pl.BlockSpec((1, seq_len, 2 * HEAD_DIM), lambda b, p, q: (b, 0, p)),
            pl.BlockSpec((1, 2 * HEAD_DIM, seq_len), lambda b, p, q: (b, p, 0)),
            pl.BlockSpec((blk, 2 * blk), lambda b, p, q: (0, 0)),
        ],
        out_specs=pl.BlockSpec((1, 2 * HEAD_DIM, blk * tiles), lambda b, p, q: (b, p, q)),
        out_shape=jax.ShapeDtypeStruct((batch, d, seq_len), F32),
        compiler_params=_params("arbitrary", "arbitrary", "arbitrary"),
        name="stick_breaking_attention",
    )(qt_pad, k3, vt_perm, _causal_bias(blk))


def _out_proj_kernel(ot_ref, w_ref, h_ref, o_ref):
    a = ot_ref[0].T.astype(BF16)
    o_ref[...] = h_ref[...] + jnp.dot(a, w_ref[...], preferred_element_type=F32)


def _out_proj(ot, w_o, h, seq_len):
    n, d = h.shape
    t = min(PROJ_TOKENS, seq_len)
    tiles = seq_len // t
    return pl.pallas_call(
        _out_proj_kernel,
        grid=(n // t,),
        in_specs=[
            pl.BlockSpec((1, d, t), lambda i: (i // tiles, 0, i % tiles)),
            pl.BlockSpec((d, d), lambda i: (0, 0)),
            pl.BlockSpec((t, d), lambda i: (i, 0)),
        ],
        out_specs=pl.BlockSpec((t, d), lambda i: (i, 0)),
        out_shape=jax.ShapeDtypeStruct((n, d), F32),
        compiler_params=_params("arbitrary"),
        name="attn_out_proj",
    )(ot, w_o.astype(BF16), h)


def kernel(x, norm_mix, norm_ffn, conv_w_in, conv_kernel, conv_w_out, norm_kv, w_kv,
           attn_w_q, attn_w_o, peer_w_query, peer_subkeys, peer_u, peer_v, norm_final):
    batch, seq_len, d = x.shape
    assert norm_mix.shape[0] == 2 and conv_w_in.shape[0] == 1 and attn_w_q.shape[0] == 1
    h = x.reshape(batch * seq_len, d)
    h = _conv_mixer(h, norm_mix[0], conv_w_in[0], conv_kernel[0], conv_w_out[0], seq_len)
    h = _peer_ffn(h, norm_ffn[0], peer_w_query[0], peer_subkeys[0], peer_u[0], peer_v[0],
                  norm_final, final_norm=False)
    k_perm, vt_perm, qt_pad = _kvq_proj(h, norm_kv, norm_mix[1], w_kv, attn_w_q[0],
                                        batch, seq_len)
    ot = _stick_breaking_attention(qt_pad, k_perm, vt_perm, seq_len)
    h = _out_proj(ot, attn_w_o[0], h, seq_len)
    h = _peer_ffn(h, norm_ffn[1], peer_w_query[1], peer_subkeys[1], peer_u[1], peer_v[1],
                  norm_final, final_norm=True)
    return h.reshape(batch, seq_len, d)
```
